```python
import jax, jax.numpy as jnp
from jax import lax
import numpy as np

D_MODEL = 1024
BATCH = 8
SEQ = 4096
DEPTH = 4

EPS = 1e-6
D_FF = 2816
HEAD_DIM = 64
CONV_DIM = D_MODEL // 2
CONV_WIDTH = 3
NSA_HEADS = (D_MODEL // 2) // HEAD_DIM
NSA_KV_HEADS = NSA_HEADS // 4
NSA_GROUP = NSA_HEADS // NSA_KV_HEADS
NSA_Q_DIM = NSA_HEADS * HEAD_DIM
NSA_KV_DIM = NSA_KV_HEADS * HEAD_DIM
CMP_BLOCK = 32
CMP_STRIDE = 16
CMP_HIDDEN = 128
SEL_BLOCK = 64
SEL_TOPK = 16
WINDOW = 512
NSA_Q_BLOCK = 64
FORCE_BONUS = 1e4
SB_HEADS = D_MODEL // HEAD_DIM
SB_DIM = SB_HEADS * HEAD_DIM
SB_Q_BLOCK = 128
AB_SPLITS = [CONV_DIM] * 3 + [NSA_Q_DIM] + [NSA_KV_DIM] * 6 + [3 * NSA_HEADS]
AB_IN_DIM = sum(AB_SPLITS)
AB_OUT_DIM = CONV_DIM + NSA_Q_DIM
NEG = -1e30

kernel_name = 'hybrid_conv_nsa_stickbreak_macaron'


def rms_norm(x, g):
    xf = x.astype(jnp.float32)
    y = xf * lax.rsqrt(jnp.mean(xf * xf, axis=-1, keepdims=True) + EPS)
    return (y * g.astype(jnp.float32)).astype(x.dtype)


def swiglu(x, w_in, w_out):
    gate, up = jnp.split(x @ w_in, 2, axis=-1)
    return (jax.nn.silu(gate) * up) @ w_out


def masked_softmax(s, mask):
    p = jax.nn.softmax(jnp.where(mask, s, NEG), axis=-1)
    return jnp.where(mask, p, 0.0)


def short_conv(b_gate, c_gate, h, conv_w):
    u = c_gate * h
    y = lax.conv_general_dilated(
        u, conv_w[:, None, :].astype(u.dtype), window_strides=(1,),
        padding=[(CONV_WIDTH - 1, 0)], dimension_numbers=('NWC', 'WIO', 'NWC'),
        feature_group_count=u.shape[-1])
    return b_gate * y


def compress_blocks(kv, pe, w1, w2):
    B, T, G, d = kv.shape
    nc = (T - CMP_BLOCK) // CMP_STRIDE + 1
    idx = jnp.arange(nc)[:, None] * CMP_STRIDE + jnp.arange(CMP_BLOCK)[None, :]
    blocks = kv[:, idx] + pe[:, None, :]
    flat = blocks.transpose(0, 1, 3, 2, 4).reshape(B, nc, G, CMP_BLOCK * d)
    return jax.nn.gelu(flat @ w1) @ w2


def cmp_to_sel_overlap(nc, ns):
    c0 = jnp.arange(nc) * CMP_STRIDE
    s0 = jnp.arange(ns) * SEL_BLOCK
    lo = jnp.maximum(c0[:, None], s0[None, :])
    hi = jnp.minimum(c0[:, None] + CMP_BLOCK, s0[None, :] + SEL_BLOCK)
    return jnp.maximum(hi - lo, 0).astype(jnp.float32) / CMP_BLOCK


def nsa_attention(q, kc, vc, ks, vs, kw, vw, gates):
    B, T, G, R, d = q.shape
    scale = d ** -0.5
    nc = kc.shape[1]
    ns = T // SEL_BLOCK
    topk = min(SEL_TOPK, ns)
    cmp_end = jnp.arange(nc) * CMP_STRIDE + CMP_BLOCK - 1
    overlap = cmp_to_sel_overlap(nc, ns)
    ks_blk = ks.reshape(B, ns, SEL_BLOCK, G, d).transpose(0, 3, 1, 2, 4)
    vs_blk = vs.reshape(B, ns, SEL_BLOCK, G, d).transpose(0, 3, 1, 2, 4)
    pad = ((0, 0), (WINDOW, 0), (0, 0), (0, 0))
    kw_pad = jnp.pad(kw, pad)
    vw_pad = jnp.pad(vw, pad)
    b_idx = jnp.arange(B)[:, None, None, None]
    g_idx = jnp.arange(G)[None, :, None, None]
    blk = jnp.arange(ns)
    n_sel = topk * SEL_BLOCK

    def block(n):
        start = n * NSA_Q_BLOCK
        qb = lax.dynamic_slice_in_dim(q, start, NSA_Q_BLOCK, axis=1)
        gb = lax.dynamic_slice_in_dim(gates, start, NSA_Q_BLOCK, axis=1)
        pos = start + jnp.arange(NSA_Q_BLOCK)
        s = jnp.einsum('bqgrd,bcgd->bgrqc', qb, kc, preferred_element_type=jnp.float32) * scale
        p_cmp = masked_softmax(s, cmp_end[None, :] <= pos[:, None])
        o_cmp = jnp.einsum('bgrqc,bcgd->bqgrd', p_cmp.astype(vc.dtype), vc)
        imp = jnp.einsum('bgrqc,cs->bgqs', p_cmp, overlap)
        cur = pos[:, None] // SEL_BLOCK
        forced = (blk[None, :] == 0) | (blk[None, :] == cur) | (blk[None, :] == cur - 1)
        valid = blk[None, :] * SEL_BLOCK <= pos[:, None]
        score = jnp.where(valid, imp + jnp.where(forced, FORCE_BONUS, 0.0), NEG)
        _, sel = lax.top_k(score, topk)
        k_sel = ks_blk[b_idx, g_idx, sel].reshape(B, G, NSA_Q_BLOCK, n_sel, d)
        v_sel = vs_blk[b_idx, g_idx, sel].reshape(B, G, NSA_Q_BLOCK, n_sel, d)
        key_pos = (sel[..., None] * SEL_BLOCK + jnp.arange(SEL_BLOCK)).reshape(B, G, NSA_Q_BLOCK, n_sel)
        s = jnp.einsum('bqgrd,bgqnd->bgrqn', qb, k_sel, preferred_element_type=jnp.float32) * scale
        p = masked_softmax(s, (key_pos <= pos[None, None, :, None])[:, :, None])
        o_sel = jnp.einsum('bgrqn,bgqnd->bqgrd', p.astype(v_sel.dtype), v_sel)
        kwb = lax.dynamic_slice_in_dim(kw_pad, start, WINDOW + NSA_Q_BLOCK, axis=1)
        vwb = lax.dynamic_slice_in_dim(vw_pad, start, WINDOW + NSA_Q_BLOCK, axis=1)
        kp = start - WINDOW + jnp.arange(WINDOW + NSA_Q_BLOCK)
        wmask = (kp[None, :] <= pos[:, None]) & (kp[None, :] > pos[:, None] - WINDOW) & (kp[None, :] >= 0)
        s = jnp.einsum('bqgrd,bkgd->bgrqk', qb, kwb, preferred_element_type=jnp.float32) * scale
        p = masked_softmax(s, wmask)
        o_win = jnp.einsum('bgrqk,bkgd->bqgrd', p.astype(vwb.dtype), vwb)
        return gb[..., 0:1] * o_cmp + gb[..., 1:2] * o_sel + gb[..., 2:3] * o_win

    out = lax.map(block, jnp.arange(T // NSA_Q_BLOCK))
    return jnp.moveaxis(out, 0, 1).reshape(B, T, G * R * d)


def conv_nsa_mixer(h, w_in, conv_w, pe_k, w1_k, w2_k, pe_v, w1_v, w2_v, w_out):
    B, T, _ = h.shape
    offs = np.cumsum(AB_SPLITS)[:-1].tolist()
    (b_gate, c_gate, hc, q, k_cmp, v_cmp, k_sel, v_sel, k_win, v_win,
     g) = jnp.split(h @ w_in, offs, axis=-1)
    y_conv = short_conv(b_gate, c_gate, hc, conv_w)
    kvs = lambda t: t.reshape(B, T, NSA_KV_HEADS, HEAD_DIM)
    kc = compress_blocks(kvs(k_cmp), pe_k, w1_k, w2_k)
    vc = compress_blocks(kvs(v_cmp), pe_v, w1_v, w2_v)
    qh = q.reshape(B, T, NSA_KV_HEADS, NSA_GROUP, HEAD_DIM)
    gates = jax.nn.sigmoid(g).reshape(B, T, NSA_KV_HEADS, NSA_GROUP, 3)
    y_nsa = nsa_attention(qh, kc, vc, kvs(k_sel), kvs(v_sel), kvs(k_win), kvs(v_win), gates)
    return jnp.concatenate([y_conv, y_nsa], axis=-1) @ w_out


def stick_breaking_attention(q, k, v):
    B, T, H, d = q.shape
    scale = d ** -0.5
    kpos = jnp.arange(T)

    def block(n):
        start = n * SB_Q_BLOCK
        qb = lax.dynamic_slice_in_dim(q, start, SB_Q_BLOCK, axis=1)
        qpos = start + jnp.arange(SB_Q_BLOCK)
        z = jnp.einsum('bqhd,bkhd->bhqk', qb, k, preferred_element_type=jnp.float32) * scale
        mask = kpos[None, :] < qpos[:, None]
        log_stay = jnp.where(mask, jax.nn.log_sigmoid(-z), 0.0)
        between = lax.cumsum(log_stay, axis=3, reverse=True) - log_stay
        a = jnp.where(mask, jnp.exp(jax.nn.log_sigmoid(z) + between), 0.0)
        return jnp.einsum('bhqk,bkhd->bqhd', a.astype(v.dtype), v)

    out = lax.map(block, jnp.arange(T // SB_Q_BLOCK))
    return jnp.moveaxis(out, 0, 1).reshape(B, T, H * d)


def stick_breaking_mixer(h, w_qkv, w_out):
    B, T, _ = h.shape
    q, k, v = jnp.split(h @ w_qkv, 3, axis=-1)
    hd = lambda t: t.reshape(B, T, SB_HEADS, HEAD_DIM)
    return stick_breaking_attention(hd(q), hd(k), hd(v)) @ w_out


def setup_inputs(seed: int = 0) -> dict:
    key = jax.random.key(seed)
    k = jax.random.split(key, 20)
    n_even = (DEPTH + 1) // 2
    n_odd = DEPTH // 2
    nrm = lambda i, shape, s: jax.random.normal(k[i], shape, jnp.float32) * s
    gain = lambda i, shape: 1.0 + 0.02 * jax.random.normal(k[i], shape, jnp.float32)
    L = CMP_BLOCK * HEAD_DIM
    return {
        'x': nrm(0, (BATCH, SEQ, D_MODEL), 1.0),
        'norm_ffn1': gain(1, (DEPTH, D_MODEL)),
        'w_ffn1_in': nrm(2, (DEPTH, D_MODEL, 2 * D_FF), D_MODEL ** -0.5),
        'w_ffn1_out': nrm(3, (DEPTH, D_FF, D_MODEL), D_FF ** -0.5),
        'norm_mix': gain(4, (DEPTH, D_MODEL)),
        'w_in_ab': nrm(5, (n_even, D_MODEL, AB_IN_DIM), D_MODEL ** -0.5),
        'conv_w': nrm(6, (n_even, CONV_WIDTH, CONV_DIM), CONV_WIDTH ** -0.5),
        'cmp_pe_k': nrm(7, (n_even, CMP_BLOCK, HEAD_DIM), 0.1),
        'cmp_w1_k': nrm(8, (n_even, L, CMP_HIDDEN), L ** -0.5),
        'cmp_w2_k': nrm(9, (n_even, CMP_HIDDEN, HEAD_DIM), CMP_HIDDEN ** -0.5),
        'cmp_pe_v': nrm(10, (n_even, CMP_BLOCK, HEAD_DIM), 0.1),
        'cmp_w1_v': nrm(11, (n_even, L, CMP_HIDDEN), L ** -0.5),
        'cmp_w2_v': nrm(12, (n_even, CMP_HIDDEN, HEAD_DIM), CMP_HIDDEN ** -0.5),
        'w_out_ab': nrm(13, (n_even, AB_OUT_DIM, D_MODEL), AB_OUT_DIM ** -0.5),
        'w_qkv_sb': nrm(14, (n_odd, D_MODEL, 3 * SB_DIM), D_MODEL ** -0.5),
        'w_out_sb': nrm(15, (n_odd, SB_DIM, D_MODEL), SB_DIM ** -0.5),
        'norm_ffn2': gain(16, (DEPTH, D_MODEL)),
        'w_ffn2_in': nrm(17, (DEPTH, D_MODEL, 2 * D_FF), D_MODEL ** -0.5),
        'w_ffn2_out': nrm(18, (DEPTH, D_FF, D_MODEL), D_FF ** -0.5),
        'norm_final': gain(19, (D_MODEL,)),
    }


def reference(x, norm_ffn1, w_ffn1_in, w_ffn1_out, norm_mix, w_in_ab, conv_w,
              cmp_pe_k, cmp_w1_k, cmp_w2_k, cmp_pe_v, cmp_w1_v, cmp_w2_v, w_out_ab,
              w_qkv_sb, w_out_sb, norm_ffn2, w_ffn2_in, w_ffn2_out, norm_final):
    for layer in range(DEPTH):
        x = x + 0.5 * swiglu(rms_norm(x, norm_ffn1[layer]), w_ffn1_in[layer], w_ffn1_out[layer])
        h = rms_norm(x, norm_mix[layer])
        i = layer // 2
        if layer % 2 == 0:
            x = x + conv_nsa_mixer(h, w_in_ab[i], conv_w[i], cmp_pe_k[i], cmp_w1_k[i],
                                   cmp_w2_k[i], cmp_pe_v[i], cmp_w1_v[i], cmp_w2_v[i],
                                   w_out_ab[i])
        else:
            x = x + stick_breaking_mixer(h, w_qkv_sb[i], w_out_sb[i])
        x = x + 0.5 * swiglu(rms_norm(x, norm_ffn2[layer]), w_ffn2_in[layer], w_ffn2_out[layer])
    return rms_norm(x, norm_final)
```

```python
import functools

import jax
import jax.numpy as jnp
from jax import lax
from jax.experimental import pallas as pl
from jax.experimental.pallas import tpu as pltpu

F32 = jnp.float32
BF16 = jnp.bfloat16

EPS = 1e-6
NEG = -1e30
HEAD_DIM = 64
Q_SCALE = HEAD_DIM ** -0.5
CONV_WIDTH = 3
NSA_KV_HEADS = 2
NSA_GROUP = 4
CMP_BLOCK = 32
CMP_STRIDE = 16
SEL_BLOCK = 64
SEL_TOPK = 16
WINDOW = 512
FORCE_BONUS = 1e4
LANES = 128
VMEM_LIMIT = 56 * 1024 * 1024

_NT = (((1,), (1,)), ((), ()))


def _params(*sem):
    return pltpu.CompilerParams(dimension_semantics=sem, vmem_limit_bytes=VMEM_LIMIT)


def _rms(x, g):
    ms = jnp.mean(x * x, axis=-1, keepdims=True)
    return x * lax.rsqrt(ms + EPS) * g


def _dot(a, b):
    return jnp.dot(a, b, preferred_element_type=F32)


def _ffn_kernel(x_ref, g_ref, wg_ref, wu_ref, wo_ref, o_ref, h_ref, acc_ref):
    j = pl.program_id(1)

    @pl.when(j == 0)
    def _():
        h_ref[...] = _rms(x_ref[...], g_ref[...]).astype(BF16)
        acc_ref[...] = jnp.zeros_like(acc_ref)

    h = h_ref[...]
    gate = _dot(h, wg_ref[...])
    up = _dot(h, wu_ref[...])
    act = gate * jax.nn.sigmoid(gate) * up
    acc_ref[...] += _dot(act.astype(BF16), wo_ref[...])

    @pl.when(j == pl.num_programs(1) - 1)
    def _():
        o_ref[...] = x_ref[...] + 0.5 * acc_ref[...]


def _ffn(x, g, w_in, w_out, tm=512):
    n, d = x.shape
    f = w_out.shape[0]
    tf = f // 2 if (f // 2) % LANES == 0 else f
    nf = f // tf
    return pl.pallas_call(
        _ffn_kernel,
        grid=(n // tm, nf),
        in_specs=[
            pl.BlockSpec((tm, d), lambda i, j: (i, 0)),
            pl.BlockSpec((1, d), lambda i, j: (0, 0)),
            pl.BlockSpec((d, tf), lambda i, j: (0, j)),
            pl.BlockSpec((d, tf), lambda i, j: (0, j + nf)),
            pl.BlockSpec((tf, d), lambda i, j: (j, 0)),
        ],
        out_specs=pl.BlockSpec((tm, d), lambda i, j: (i, 0)),
        out_shape=jax.ShapeDtypeStruct((n, d), F32),
        scratch_shapes=[pltpu.VMEM((tm, d), BF16), pltpu.VMEM((tm, d), F32)],
        compiler_params=_params("parallel", "arbitrary"),
        name="ffn",
    )(x, g.reshape(1, d), w_in, w_in, w_out)


def _proj_kernel(x_ref, g_ref, w_ref, *o_refs):
    h = _rms(x_ref[...], g_ref[...]).astype(BF16)
    off = 0
    for o_ref in o_refs:
        width = o_ref.shape[1]
        o_ref[...] = _dot(h, w_ref[:, off:off + width]).astype(o_ref.dtype)
        off += width


def _norm_proj(x, g, w, outs, tm=512):
    n, d = x.shape
    return pl.pallas_call(
        _proj_kernel,
        grid=(n // tm,),
        in_specs=[
            pl.BlockSpec((tm, d), lambda i: (i, 0)),
            pl.BlockSpec((1, d), lambda i: (0, 0)),
            pl.BlockSpec(w.shape, lambda i: (0, 0)),
        ],
        out_specs=[pl.BlockSpec((tm, wd), lambda i: (i, 0)) for wd, _ in outs],
        out_shape=[jax.ShapeDtypeStruct((n, wd), dt) for wd, dt in outs],
        compiler_params=_params("parallel"),
        name="norm_proj",
    )(x, g.reshape(1, d), w)


def _conv_kernel(cv_ref, prev_ref, w_ref, o_ref, ext_ref, *, tiles_per_seq, cd):
    tm = o_ref.shape[0]
    first = (pl.program_id(0) % tiles_per_seq) == 0
    u = cv_ref[:, cd:2 * cd] * cv_ref[:, 2 * cd:3 * cd]
    u_prev = prev_ref[:, cd:2 * cd] * prev_ref[:, 2 * cd:3 * cd]
    ext_ref[0:8, :] = jnp.where(first, 0.0, u_prev)
    ext_ref[8:, :] = u
    w = w_ref[...]
    y = w[0:1] * ext_ref[6:6 + tm, :] + w[1:2] * ext_ref[7:7 + tm, :] + w[2:3] * u
    o_ref[...] = (cv_ref[:, 0:cd] * y).astype(o_ref.dtype)


def _short_conv(cv, conv_w, seq, tm=512):
    n = cv.shape[0]
    cd = conv_w.shape[1]
    kern = functools.partial(_conv_kernel, tiles_per_seq=seq // tm, cd=cd)
    return pl.pallas_call(
        kern,
        grid=(n // tm,),
        in_specs=[
            pl.BlockSpec((tm, 3 * cd), lambda i: (i, 0)),
            pl.BlockSpec((8, 3 * cd), lambda i: (jnp.maximum(i * (tm // 8) - 1, 0), 0)),
            pl.BlockSpec(conv_w.shape, lambda i: (0, 0)),
        ],
        out_specs=pl.BlockSpec((tm, cd), lambda i: (i, 0)),
        out_shape=jax.ShapeDtypeStruct((n, cd), BF16),
        scratch_shapes=[pltpu.VMEM((tm + 8, cd), F32)],
        compiler_params=_params("parallel"),
        name="short_conv",
    )(cv, cv, conv_w)


def _compress_kernel(cm_ref, w1_ref, w2_ref, pe_ref, o_ref):
    nch = o_ref.shape[0]
    half = cm_ref.shape[1]
    outs = []
    for t in range(2):
        cm = cm_ref[t * nch:(t + 1) * nch, :]
        a = _dot(cm, w1_ref[t, 0:half, :])
        b = _dot(cm, w1_ref[t, half:2 * half, :])
        pe = jnp.broadcast_to(pe_ref[t], (8, 2 * half)).astype(BF16)
        bias = _dot(pe, w1_ref[t])[0:1]
        hid = a + pltpu.roll(b, nch - 1, 0) + bias
        outs.append(_dot(jax.nn.gelu(hid).astype(BF16), w2_ref[t]))
    o_ref[...] = jnp.concatenate(outs, axis=1).astype(o_ref.dtype)


def _compress(cm, w1, w2, pe, nch):
    rows, half = cm.shape
    bg = rows // (2 * nch)
    d = w2.shape[2]
    return pl.pallas_call(
        _compress_kernel,
        grid=(bg,),
        in_specs=[
            pl.BlockSpec((2 * nch, half), lambda i: (i, 0)),
            pl.BlockSpec(w1.shape, lambda i: (0, 0, 0)),
            pl.BlockSpec(w2.shape, lambda i: (0, 0, 0)),
            pl.BlockSpec(pe.shape, lambda i: (0, 0, 0)),
        ],
        out_specs=pl.BlockSpec((nch, 2 * d), lambda i: (i, 0)),
        out_shape=jax.ShapeDtypeStruct((bg * nch, 2 * d), BF16),
        compiler_params=_params("parallel"),
        name="compress",
    )(cm, w1, w2, pe)


def _cmp_kernel(q_ref, kvc_ref, ovl_ref, gt_ref, o_ref, sel_ref, *, nblk):
    tq = q_ref.shape[0]
    nc = kvc_ref.shape[0]
    d = HEAD_DIM
    i = pl.program_id(2)
    q = q_ref[...]
    kvc = kvc_ref[...]
    kc = kvc[:, 0:d]
    pos = i * tq + lax.broadcasted_iota(jnp.int32, (tq, nc), 0)
    cidx = lax.broadcasted_iota(jnp.int32, (tq, nc), 1)
    cmask = cidx * CMP_STRIDE + (CMP_BLOCK - 1) <= pos
    gate = jax.nn.sigmoid(gt_ref[...])
    imp = jnp.zeros((tq, LANES), F32)
    outs = []
    for r in range(NSA_GROUP):
        s = lax.dot_general(q[:, r * d:(r + 1) * d], kc, _NT, preferred_element_type=F32)
        s = jnp.where(cmask, s, NEG)
        e = jnp.exp(s - jnp.max(s, axis=1, keepdims=True))
        p = jnp.where(cmask, e / jnp.sum(e, axis=1, keepdims=True), 0.0)
        pb = p.astype(BF16)
        o = _dot(pb, kvc)[:, d:2 * d]
        outs.append(o * gate[:, 3 * r:3 * r + 1])
        imp = imp + _dot(pb, ovl_ref[...])
    o_ref[...] = jnp.concatenate(outs, axis=1)

    posb = i * tq + lax.broadcasted_iota(jnp.int32, (tq, LANES), 0)
    blk = lax.broadcasted_iota(jnp.int32, (tq, LANES), 1)
    cur = posb >> 6
    forced = jnp.where(blk == 0, 1.0, jnp.where(blk == cur, 1.0, jnp.where(blk == cur - 1, 1.0, 0.0)))
    score = jnp.where(blk * SEL_BLOCK <= posb, imp + forced * FORCE_BONUS, NEG)
    sc = score.T[0:nblk, :]
    sidx = lax.broadcasted_iota(jnp.int32, (nblk, tq), 0)
    rank = jnp.zeros((nblk, tq), F32)
    for sp in range(nblk):
        other = sc[sp:sp + 1, :]
        tie = jnp.where(sidx > sp, 1.0, 0.0)
        rank = rank + jnp.where(other > sc, 1.0, jnp.where(other == sc, tie, 0.0))
    chosen = jnp.where(rank < SEL_TOPK, 1.0, 0.0)
    chosen = jnp.concatenate([chosen, jnp.zeros((LANES - nblk, tq), F32)], axis=0)
    sel_ref[...] = chosen.T.astype(sel_ref.dtype)


def _cmp_branch(q, kvc, ovl, gates, batch, seq, tq=128):
    n = q.shape[0]
    nt = seq // tq
    g = NSA_KV_HEADS
    wq = NSA_GROUP * HEAD_DIM
    nc = kvc.shape[0] // (batch * g)
    kern = functools.partial(_cmp_kernel, nblk=seq // SEL_BLOCK)
    return pl.pallas_call(
        kern,
        grid=(batch, g, nt),
        in_specs=[
            pl.BlockSpec((tq, wq), lambda b, h, i: (b * nt + i, h)),
            pl.BlockSpec((nc, 2 * HEAD_DIM), lambda b, h, i: (b * g + h, 0)),
            pl.BlockSpec(ovl.shape, lambda b, h, i: (0, 0)),
            pl.BlockSpec((tq, LANES), lambda b, h, i: (b * nt + i, h)),
        ],
        out_specs=[
            pl.BlockSpec((tq, wq), lambda b, h, i: (b * nt + i, h)),
            pl.BlockSpec((tq, LANES), lambda b, h, i: ((b * g + h) * nt + i, 0)),
        ],
        out_shape=[
            jax.ShapeDtypeStruct((n, g * wq), F32),
            jax.ShapeDtypeStruct((batch * g * seq, LANES), BF16),
        ],
        compiler_params=_params("parallel", "parallel", "parallel"),
        name="nsa_cmp_select",
    )(q, kvc, ovl, gates)


def _nsa_attn_kernel(q_ref, kv_ref, sel_ref, gt_ref, o_ref, m_ref, l_ref, acc_ref, *, mode, branch):
    tq = q_ref.shape[0]
    tk = tq
    d = HEAD_DIM
    i = pl.program_id(2)
    m_ref[...] = jnp.full(m_ref.shape, NEG, F32)
    l_ref[...] = jnp.zeros_like(l_ref)
    acc_ref[...] = jnp.zeros_like(acc_ref)
    q = q_ref[...]
    qs = [q[:, r * d:(r + 1) * d] for r in range(NSA_GROUP)]
    row = lax.broadcasted_iota(jnp.int32, (tq, tk), 0)
    col = lax.broadcasted_iota(jnp.int32, (tq, tk), 1)

    def tile(kt, mask):
        kv = kv_ref[pl.ds(pl.multiple_of(kt * tk, tk), tk), :]
        k = kv[:, 0:d]
        for r in range(NSA_GROUP):
            s = lax.dot_general(qs[r], k, _NT, preferred_element_type=F32)
            if mask is not None:
                s = jnp.where(mask, s, NEG)
            m_prev = m_ref[r]
            m_new = jnp.maximum(m_prev, jnp.max(s, axis=1, keepdims=True))
            alpha = jnp.exp(m_prev - m_new)
            p = jnp.exp(s - m_new)
            if mask is not None:
                p = jnp.where(mask, p, 0.0)
            l_ref[r] = alpha * l_ref[r] + jnp.sum(p, axis=1, keepdims=True)
            acc_ref[r] = alpha * acc_ref[r] + _dot(p.astype(BF16), kv)
            m_ref[r] = m_new

    if mode == "sel":
        srow = lax.broadcasted_iota(jnp.int32, (LANES, tk), 0)
        scol = lax.broadcasted_iota(jnp.int32, (LANES, tk), 1) >> 6
        sel = sel_ref[...]

        def sel_tile(kt, diag):
            expand = jnp.where(srow == kt * (tk // SEL_BLOCK) + scol, 1.0, 0.0).astype(BF16)
            chosen = _dot(sel, expand)
            if diag:
                chosen = jnp.where(col <= row, chosen, 0.0)
            tile(kt, chosen > 0.5)

        def body(kt, c):
            sel_tile(kt, False)
            return c

        lax.fori_loop(0, i, body, 0)
        sel_tile(i, True)
    else:
        @pl.when(i >= 2)
        def _():
            tile(i - 2, col > row)

        @pl.when(i >= 1)
        def _():
            tile(i - 1, None)

        tile(i, col <= row)

    gate = jax.nn.sigmoid(gt_ref[...])
    outs = []
    for r in range(NSA_GROUP):
        o = acc_ref[r][:, d:2 * d] / l_ref[r]
        outs.append(o * gate[:, 3 * r + branch:3 * r + branch + 1])
    o_ref[...] = jnp.concatenate(outs, axis=1)


def _nsa_attn(q, kv, sel, gates, batch, seq, mode, tq=256):
    n = q.shape[0]
    nt = seq // tq
    g = NSA_KV_HEADS
    wq = NSA_GROUP * HEAD_DIM
    branch = 1 if mode == "sel" else 2
    assert WINDOW == 2 * tq
    kern = functools.partial(_nsa_attn_kernel, mode=mode, branch=branch)
    return pl.pallas_call(
        kern,
        grid=(batch, g, nt),
        in_specs=[
            pl.BlockSpec((tq, wq), lambda b, h, i: (b * nt + i, h)),
            pl.BlockSpec((seq, 2 * HEAD_DIM), lambda b, h, i: (b, h * 3 + branch)),
            pl.BlockSpec((tq, LANES), lambda b, h, i: ((b * g + h) * nt + i, 0)),
            pl.BlockSpec((tq, LANES), lambda b, h, i: (b * nt + i, h)),
        ],
        out_specs=pl.BlockSpec((tq, wq), lambda b, h, i: (b * nt + i, h)),
        out_shape=jax.ShapeDtypeStruct((n, g * wq), F32),
        scratch_shapes=[
            pltpu.VMEM((NSA_GROUP, tq, 1), F32),
            pltpu.VMEM((NSA_GROUP, tq, 1), F32),
            pltpu.VMEM((NSA_GROUP, tq, 2 * HEAD_DIM), F32),
        ],
        compiler_params=_params("parallel", "parallel", "parallel"),
        name="nsa_" + mode,
    )(q, kv, sel, gates)


def _sb_kernel(q_ref, k_ref, v_ref, o_ref, acc_ref, carry_ref):
    tq = q_ref.shape[0]
    tk = tq
    d = HEAD_DIM
    i = pl.program_id(2)
    row = lax.broadcasted_iota(jnp.int32, (tq, tk), 0)
    col = lax.broadcasted_iota(jnp.int32, (tq, tk), 1)
    later = jnp.where(row > col, 1.0, 0.0).astype(BF16)
    causal = col < row
    q = q_ref[...]
    acc_ref[...] = jnp.zeros_like(acc_ref)
    carry_ref[...] = jnp.zeros_like(carry_ref)

    def tile(kt, diag):
        start = pl.multiple_of(kt * tk, tk)
        k = k_ref[pl.ds(start, tk), :]
        v = v_ref[pl.ds(start, tk), :]
        for h in range(2):
            z = lax.dot_general(q[:, h * d:(h + 1) * d], k[:, h * d:(h + 1) * d], _NT,
                                preferred_element_type=F32)
            sp = jnp.maximum(z, 0.0) + jnp.log1p(jnp.exp(-jnp.abs(z)))
            log_stay = -sp
            if diag:
                log_stay = jnp.where(causal, log_stay, 0.0)
            hi = log_stay.astype(BF16)
            lo = (log_stay - hi.astype(F32)).astype(BF16)
            between = _dot(hi, later) + _dot(lo, later) + carry_ref[h]
            a = jnp.exp(z - sp + between)
            if diag:
                a = jnp.where(causal, a, 0.0)
            acc_ref[h] += _dot(a.astype(BF16), v)
            carry_ref[h] += jnp.sum(log_stay, axis=1, keepdims=True)

    tile(i, True)

    def body(s, c):
        tile(i - 1 - s, False)
        return c

    lax.fori_loop(0, i, body, 0)
    lane = lax.broadcasted_iota(jnp.int32, (tq, 2 * d), 1)
    o_ref[...] = jnp.where(lane < d, acc_ref[0], acc_ref[1]).astype(o_ref.dtype)


def _stick_breaking(qkv, batch, seq, heads, tq=256):
    n = qkv.shape[0]
    nt = seq // tq
    pairs = heads // 2
    return pl.pallas_call(
        _sb_kernel,
        grid=(batch, pairs, nt),
        in_specs=[
            pl.BlockSpec((tq, LANES), lambda b, h, i: (b * nt + i, h)),
            pl.BlockSpec((seq, LANES), lambda b, h, i: (b, pairs + h)),
            pl.BlockSpec((seq, LANES), lambda b, h, i: (b, 2 * pairs + h)),
        ],
        out_specs=pl.BlockSpec((tq, LANES), lambda b, h, i: (b * nt + i, h)),
        out_shape=jax.ShapeDtypeStruct((n, heads * HEAD_DIM), BF16),
        scratch_shapes=[
            pltpu.VMEM((2, tq, LANES), F32),
            pltpu.VMEM((2, tq, 1), F32),
        ],
        compiler_params=_params("parallel", "parallel", "parallel"),
        name="stick_breaking",
    )(qkv, qkv, qkv)


def _out_even_kernel(x_ref, yc_ref, oc_ref, os_ref, ow_ref, w_ref, o_ref):
    cd = yc_ref.shape[1]
    y_nsa = (oc_ref[...] + os_ref[...] + ow_ref[...]).astype(BF16)
    o_ref[...] = x_ref[...] + _dot(yc_ref[...], w_ref[0:cd, :]) + _dot(y_nsa, w_ref[cd:, :])


def _out_odd_kernel(x_ref, y_ref, w_ref, o_ref):
    o_ref[...] = x_ref[...] + _dot(y_ref[...], w_ref[...])


def _out_proj(kern, x, ys, w, tm=512):
    n, d = x.shape
    return pl.pallas_call(
        kern,
        grid=(n // tm,),
        in_specs=[pl.BlockSpec((tm, d), lambda i: (i, 0))]
        + [pl.BlockSpec((tm, y.shape[1]), lambda i: (i, 0)) for y in ys]
        + [pl.BlockSpec(w.shape, lambda i: (0, 0))],
        out_specs=pl.BlockSpec((tm, d), lambda i: (i, 0)),
        out_shape=jax.ShapeDtypeStruct((n, d), F32),
        compiler_params=_params("parallel"),
        name="out_proj",
    )(x, *ys, w)


def _final_norm_kernel(x_ref, g_ref, o_ref):
    o_ref[...] = _rms(x_ref[...], g_ref[...])


def _final_norm(x, g, tm=512):
    n, d = x.shape
    return pl.pallas_call(
        _final_norm_kernel,
        grid=(n // tm,),
        in_specs=[pl.BlockSpec((tm, d), lambda i: (i, 0)), pl.BlockSpec((1, d), lambda i: (0, 0))],
        out_specs=pl.BlockSpec((tm, d), lambda i: (i, 0)),
        out_shape=jax.ShapeDtypeStruct((n, d), F32),
        compiler_params=_params("parallel"),
        name="final_norm",
    )(x, g.reshape(1, d))


def _overlap_matrix(nch, nblk):
    c0 = jnp.arange(nch) * CMP_STRIDE
    s0 = jnp.arange(LANES) * SEL_BLOCK
    lo = jnp.maximum(c0[:, None], s0[None, :])
    hi = jnp.minimum(c0[:, None] + CMP_BLOCK, s0[None, :] + SEL_BLOCK)
    ovl = jnp.maximum(hi - lo, 0).astype(F32) / CMP_BLOCK
    keep = (jnp.arange(nch)[:, None] < nch - 1) & (jnp.arange(LANES)[None, :] < nblk)
    return jnp.where(keep, ovl, 0.0).astype(BF16)


def _conv_nsa_mixer(x, g, w_in, conv_w, pe_k, w1_k, w2_k, pe_v, w1_v, w2_v, w_out, batch, seq):
    d = HEAD_DIM
    cd = conv_w.shape[1]
    kvd = NSA_KV_HEADS * d
    o_q = 3 * cd
    o_kv = o_q + NSA_KV_HEADS * NSA_GROUP * d
    o_g = o_kv + 6 * kvd
    ng = NSA_GROUP * 3
    kv_cols = []
    gate_cols = []
    for h in range(NSA_KV_HEADS):
        for typ in range(3):
            base = o_kv + typ * 2 * kvd + h * d
            kv_cols += [w_in[:, base:base + d], w_in[:, base + kvd:base + kvd + d]]
        gate_cols += [w_in[:, o_g + h * ng:o_g + (h + 1) * ng], jnp.zeros((w_in.shape[0], LANES - ng), F32)]
    w_all = jnp.concatenate(
        [w_in[:, :o_q], w_in[:, o_q:o_kv] * Q_SCALE] + kv_cols + gate_cols, axis=1).astype(BF16)
    cv, q, kv, gates = _norm_proj(
        x, g, w_all,
        [(3 * cd, F32), (o_kv - o_q, BF16), (6 * kvd, BF16), (NSA_KV_HEADS * LANES, F32)])

    y_conv = _short_conv(cv, conv_w, seq)

    nch = seq // CMP_STRIDE
    kv6 = kv.reshape(batch, seq, NSA_KV_HEADS, 3, 2, d)
    cm = kv6[:, :, :, 0].transpose(0, 2, 3, 1, 4).reshape(batch * NSA_KV_HEADS * 2 * nch, CMP_STRIDE * d)
    w1 = jnp.stack([w1_k, w1_v]).astype(BF16)
    w2 = jnp.stack([w2_k, w2_v]).astype(BF16)
    pe = jnp.stack([pe_k.reshape(1, -1), pe_v.reshape(1, -1)])
    kvc = _compress(cm, w1, w2, pe, nch)

    ovl = _overlap_matrix(nch, seq // SEL_BLOCK)
    o_cmp, sel = _cmp_branch(q, kvc, ovl, gates, batch, seq)
    o_sel = _nsa_attn(q, kv, sel, gates, batch, seq, "sel")
    o_win = _nsa_attn(q, kv, sel, gates, batch, seq, "win")
    return _out_proj(_out_even_kernel, x, [y_conv, o_cmp, o_sel, o_win], w_out.astype(BF16))


def _stick_breaking_mixer(x, g, w_qkv, w_out, batch, seq):
    hd = w_out.shape[0]
    w = jnp.concatenate([w_qkv[:, :hd] * Q_SCALE, w_qkv[:, hd:]], axis=1).astype(BF16)
    (qkv,) = _norm_proj(x, g, w, [(3 * hd, BF16)])
    y = _stick_breaking(qkv, batch, seq, hd // HEAD_DIM)
    return _out_proj(_out_odd_kernel, x, [y], w_out.astype(BF16))


def kernel(x, norm_ffn1, w_ffn1_in, w_ffn1_out, norm_mix, w_in_ab, conv_w, cmp_pe_k, cmp_w1_k, cmp_w2_k,
           cmp_pe_v, cmp_w1_v, cmp_w2_v, w_out_ab, w_qkv_sb, w_out_sb, norm_ffn2, w_ffn2_in, w_ffn2_out,
           norm_final):
    batch, seq, d_model = x.shape
    depth = norm_ffn1.shape[0]
    x = x.reshape(batch * seq, d_model)
    for layer in range(depth):
        x = _ffn(x, norm_ffn1[layer], w_ffn1_in[layer].astype(BF16), w_ffn1_out[layer].astype(BF16))
        i = layer // 2
        if layer % 2 == 0:
            x = _conv_nsa_mixer(x, norm_mix[layer], w_in_ab[i], conv_w[i], cmp_pe_k[i], cmp_w1_k[i],
                                cmp_w2_k[i], cmp_pe_v[i], cmp_w1_v[i], cmp_w2_v[i], w_out_ab[i], batch, seq)
        else:
            x = _stick_breaking_mixer(x, norm_mix[layer], w_qkv_sb[i], w_out_sb[i], batch, seq)
        x = _ffn(x, norm_ffn2[layer], w_ffn2_in[layer].astype(BF16), w_ffn2_out[layer].astype(BF16))
    return _final_norm(x, norm_final).reshape(batch, seq, d_model)
```

```python
import functools

import jax
import jax.numpy as jnp
from jax import lax
from jax.experimental import pallas as pl
from jax.experimental.pallas import tpu as pltpu

F32 = jnp.float32
BF16 = jnp.bfloat16

EPS = 1e-6
NEG = -1e30
HEAD_DIM = 64
Q_SCALE = HEAD_DIM ** -0.5
LOG2_E = 1.4426950408889634
CONV_WIDTH = 3
NSA_KV_HEADS = 2
NSA_GROUP = 4
CMP_BLOCK = 32
CMP_STRIDE = 16
SEL_BLOCK = 64
SEL_TOPK = 16
WINDOW = 512
FORCE_BONUS = 1e4
LANES = 128
VMEM_LIMIT = 56 * 1024 * 1024

_NT = (((1,), (1,)), ((), ()))


def _params(*sem, flags=None):
    return pltpu.CompilerParams(dimension_semantics=sem, vmem_limit_bytes=VMEM_LIMIT, flags=flags)


def _rms(x, g):
    ms = jnp.mean(x * x, axis=-1, keepdims=True)
    return x * lax.rsqrt(ms + EPS) * g


def _dot(a, b):
    return jnp.dot(a, b, preferred_element_type=F32)


def _ffn_kernel(x_ref, g_ref, wg_ref, wu_ref, wo_ref, o_ref, h_ref, acc_ref):
    j = pl.program_id(1)

    @pl.when(j == 0)
    def _():
        h_ref[...] = _rms(x_ref[...], g_ref[...]).astype(BF16)
        acc_ref[...] = jnp.zeros_like(acc_ref)

    h = h_ref[...]
    gate = _dot(h, wg_ref[...])
    up = _dot(h, wu_ref[...])
    act = gate * jax.nn.sigmoid(gate) * up
    acc_ref[...] += _dot(act.astype(BF16), wo_ref[...])

    @pl.when(j == pl.num_programs(1) - 1)
    def _():
        o_ref[...] = x_ref[...] + 0.5 * acc_ref[...]


def _ffn(x, g, w_in, w_out, tm=512):
    n, d = x.shape
    f = w_out.shape[0]
    tf = f // 2 if (f // 2) % LANES == 0 else f
    nf = f // tf
    return pl.pallas_call(
        _ffn_kernel,
        grid=(n // tm, nf),
        in_specs=[
            pl.BlockSpec((tm, d), lambda i, j: (i, 0)),
            pl.BlockSpec((1, d), lambda i, j: (0, 0)),
            pl.BlockSpec((d, tf), lambda i, j: (0, j)),
            pl.BlockSpec((d, tf), lambda i, j: (0, j + nf)),
            pl.BlockSpec((tf, d), lambda i, j: (j, 0)),
        ],
        out_specs=pl.BlockSpec((tm, d), lambda i, j: (i, 0)),
        out_shape=jax.ShapeDtypeStruct((n, d), F32),
        scratch_shapes=[pltpu.VMEM((tm, d), BF16), pltpu.VMEM((tm, d), F32)],
        compiler_params=_params("parallel", "arbitrary"),
        name="ffn",
    )(x, g.reshape(1, d), w_in, w_in, w_out)


def _proj_kernel(x_ref, g_ref, w_ref, *o_refs):
    h = _rms(x_ref[...], g_ref[...]).astype(BF16)
    off = 0
    for o_ref in o_refs:
        width = o_ref.shape[1]
        o_ref[...] = _dot(h, w_ref[:, off:off + width]).astype(o_ref.dtype)
        off += width


def _norm_proj(x, g, w, outs, tm=512):
    n, d = x.shape
    return pl.pallas_call(
        _proj_kernel,
        grid=(n // tm,),
        in_specs=[
            pl.BlockSpec((tm, d), lambda i: (i, 0)),
            pl.BlockSpec((1, d), lambda i: (0, 0)),
            pl.BlockSpec(w.shape, lambda i: (0, 0)),
        ],
        out_specs=[pl.BlockSpec((tm, wd), lambda i: (i, 0)) for wd, _ in outs],
        out_shape=[jax.ShapeDtypeStruct((n, wd), dt) for wd, dt in outs],
        compiler_params=_params("parallel"),
        name="norm_proj",
    )(x, g.reshape(1, d), w)


def _conv_kernel(cv_ref, prev_ref, w_ref, o_ref, ext_ref, *, tiles_per_seq, cd):
    tm = o_ref.shape[0]
    first = (pl.program_id(0) % tiles_per_seq) == 0
    u = cv_ref[:, cd:2 * cd] * cv_ref[:, 2 * cd:3 * cd]
    u_prev = prev_ref[:, cd:2 * cd] * prev_ref[:, 2 * cd:3 * cd]
    ext_ref[0:8, :] = jnp.where(first, 0.0, u_prev)
    ext_ref[8:, :] = u
    w = w_ref[...]
    y = w[0:1] * ext_ref[6:6 + tm, :] + w[1:2] * ext_ref[7:7 + tm, :] + w[2:3] * u
    o_ref[...] = (cv_ref[:, 0:cd] * y).astype(o_ref.dtype)


def _short_conv(cv, conv_w, seq, tm=512):
    n = cv.shape[0]
    cd = conv_w.shape[1]
    kern = functools.partial(_conv_kernel, tiles_per_seq=seq // tm, cd=cd)
    return pl.pallas_call(
        kern,
        grid=(n // tm,),
        in_specs=[
            pl.BlockSpec((tm, 3 * cd), lambda i: (i, 0)),
            pl.BlockSpec((8, 3 * cd), lambda i: (jnp.maximum(i * (tm // 8) - 1, 0), 0)),
            pl.BlockSpec(conv_w.shape, lambda i: (0, 0)),
        ],
        out_specs=pl.BlockSpec((tm, cd), lambda i: (i, 0)),
        out_shape=jax.ShapeDtypeStruct((n, cd), BF16),
        scratch_shapes=[pltpu.VMEM((tm + 8, cd), F32)],
        compiler_params=_params("parallel"),
        name="short_conv",
    )(cv, cv, conv_w)


def _compress_kernel(cm_ref, w1_ref, w2_ref, pe_ref, o_ref):
    nch = o_ref.shape[0]
    half = cm_ref.shape[1]
    outs = []
    for t in range(2):
        cm = cm_ref[t * nch:(t + 1) * nch, :]
        a = _dot(cm, w1_ref[t, 0:half, :])
        b = _dot(cm, w1_ref[t, half:2 * half, :])
        pe = jnp.broadcast_to(pe_ref[t], (8, 2 * half)).astype(BF16)
        bias = _dot(pe, w1_ref[t])[0:1]
        hid = a + pltpu.roll(b, nch - 1, 0) + bias
        outs.append(_dot(jax.nn.gelu(hid).astype(BF16), w2_ref[t]))
    o_ref[...] = jnp.concatenate(outs, axis=1).astype(o_ref.dtype)


def _compress(cm, w1, w2, pe, nch):
    rows, half = cm.shape
    bg = rows // (2 * nch)
    d = w2.shape[2]
    return pl.pallas_call(
        _compress_kernel,
        grid=(bg,),
        in_specs=[
            pl.BlockSpec((2 * nch, half), lambda i: (i, 0)),
            pl.BlockSpec(w1.shape, lambda i: (0, 0, 0)),
            pl.BlockSpec(w2.shape, lambda i: (0, 0, 0)),
            pl.BlockSpec(pe.shape, lambda i: (0, 0, 0)),
        ],
        out_specs=pl.BlockSpec((nch, 2 * d), lambda i: (i, 0)),
        out_shape=jax.ShapeDtypeStruct((bg * nch, 2 * d), BF16),
        compiler_params=_params("parallel"),
        name="compress",
    )(cm, w1, w2, pe)


def _cmp_kernel(q_ref, kvc_ref, ovl_ref, gt_ref, o_ref, sel_ref, *, nblk):
    tq = q_ref.shape[0]
    nc = kvc_ref.shape[0]
    d = HEAD_DIM
    i = pl.program_id(2)
    q = q_ref[...]
    kvc = kvc_ref[...]
    kc = kvc[:, 0:d]
    pos = i * tq + lax.broadcasted_iota(jnp.int32, (tq, nc), 0)
    cidx = lax.broadcasted_iota(jnp.int32, (tq, nc), 1)
    cmask = cidx * CMP_STRIDE + (CMP_BLOCK - 1) <= pos
    gate = jax.nn.sigmoid(gt_ref[...])
    imp = jnp.zeros((tq, LANES), F32)
    outs = []
    for r in range(NSA_GROUP):
        s = lax.dot_general(q[:, r * d:(r + 1) * d], kc, _NT, preferred_element_type=F32)
        s = jnp.where(cmask, s, NEG)
        e = jnp.exp2(s - jnp.max(s, axis=1, keepdims=True))
        p = jnp.where(cmask, e / jnp.sum(e, axis=1, keepdims=True), 0.0)
        pb = p.astype(BF16)
        o = _dot(pb, kvc)[:, d:2 * d]
        outs.append(o * gate[:, 3 * r:3 * r + 1])
        imp = imp + _dot(pb, ovl_ref[...])
    o_ref[...] = jnp.concatenate(outs, axis=1)

    posb = i * tq + lax.broadcasted_iota(jnp.int32, (tq, LANES), 0)
    blk = lax.broadcasted_iota(jnp.int32, (tq, LANES), 1)
    cur = posb >> 6
    forced = jnp.where(blk == 0, 1.0, jnp.where(blk == cur, 1.0, jnp.where(blk == cur - 1, 1.0, 0.0)))
    score = jnp.where(blk * SEL_BLOCK <= posb, imp + forced * FORCE_BONUS, NEG)
    sc = score.T[0:nblk, :]
    sidx = lax.broadcasted_iota(jnp.int32, (nblk, tq), 0)
    rank = jnp.zeros((nblk, tq), F32)
    for sp in range(nblk):
        other = sc[sp:sp + 1, :]
        tie = jnp.where(sidx > sp, 1.0, 0.0)
        rank = rank + jnp.where(other > sc, 1.0, jnp.where(other == sc, tie, 0.0))
    chosen = jnp.where(rank < SEL_TOPK, 1.0, 0.0)
    chosen = jnp.concatenate([chosen, jnp.zeros((LANES - nblk, tq), F32)], axis=0)
    sel_ref[...] = chosen.T.astype(sel_ref.dtype)


def _cmp_branch(q, kvc, ovl, gates, batch, seq, tq=128):
    n = q.shape[0]
    nt = seq // tq
    g = NSA_KV_HEADS
    wq = NSA_GROUP * HEAD_DIM
    nc = kvc.shape[0] // (batch * g)
    kern = functools.partial(_cmp_kernel, nblk=seq // SEL_BLOCK)
    return pl.pallas_call(
        kern,
        grid=(batch, g, nt),
        in_specs=[
            pl.BlockSpec((tq, wq), lambda b, h, i: (b * nt + i, h)),
            pl.BlockSpec((nc, 2 * HEAD_DIM), lambda b, h, i: (b * g + h, 0)),
            pl.BlockSpec(ovl.shape, lambda b, h, i: (0, 0)),
            pl.BlockSpec((tq, LANES), lambda b, h, i: (b * nt + i, h)),
        ],
        out_specs=[
            pl.BlockSpec((tq, wq), lambda b, h, i: (b * nt + i, h)),
            pl.BlockSpec((tq, LANES), lambda b, h, i: ((b * g + h) * nt + i, 0)),
        ],
        out_shape=[
            jax.ShapeDtypeStruct((n, g * wq), F32),
            jax.ShapeDtypeStruct((batch * g * seq, LANES), BF16),
        ],
        compiler_params=_params("parallel", "parallel", "parallel"),
        name="nsa_cmp_select",
    )(q, kvc, ovl, gates)


NSA_STAGES = 3
NSA_CHUNK = 32
MASK_BIG = 1e30


def _nsa_attn_kernel(ti_ref, tj_ref, tm_ref, tf_ref, q_ref, kv_ref, *rest, mode, branch, tq, n_units):
    if mode == "sel":
        sel_ref, gt_ref, o_ref, qh_ref, va_ref, ex_ref, bias_ref, s_ref, p_ref, al_ref, m_ref, acc_ref = rest
    else:
        gt_ref, o_ref, qh_ref, va_ref, bias_ref, s_ref, p_ref, al_ref, m_ref, acc_ref = rest
    tk = tq
    d = HEAD_DIM
    seq = q_ref.shape[0]
    row = lax.broadcasted_iota(jnp.int32, (tq, tk), 0)
    col = lax.broadcasted_iota(jnp.int32, (tq, tk), 1)
    bias_ref[0] = jnp.zeros((tq, tk), F32)
    bias_ref[1] = jnp.where(col <= row, 0.0, NEG)
    bias_ref[2] = jnp.where(col > row, 0.0, NEG)
    s_ref[...] = jnp.zeros_like(s_ref)
    p_ref[...] = jnp.zeros_like(p_ref)
    al_ref[...] = jnp.zeros_like(al_ref)
    acc_ref[...] = jnp.zeros_like(acc_ref)
    m_ref[...] = jnp.full(m_ref.shape, NEG, F32)
    lane = lax.broadcasted_iota(jnp.int32, (tq, 2 * d), 1)
    srow = lax.broadcasted_iota(jnp.int32, (LANES, tk), 0)
    scol = lax.broadcasted_iota(jnp.int32, (LANES, tk), 1) >> 6

    def prep(c, carry):
        rows = pl.ds(pl.multiple_of(c * tq, tq), tq)
        q = q_ref[rows, :]
        for r in range(NSA_GROUP):
            qh_ref[r, rows, :] = q[:, r * d:(r + 1) * d]
        kv = kv_ref[rows, :]
        v_first = jnp.concatenate([kv[:, d:2 * d], kv[:, 0:d]], axis=1)
        va_ref[rows, :] = jnp.where(lane < d, v_first, 1.0).astype(BF16)
        if mode == "sel":
            ex_ref[c] = jnp.where(srow == c * (tk // SEL_BLOCK) + scol, 1.0, 0.0).astype(BF16)
        return carry

    lax.fori_loop(0, seq // tq, prep, 0)

    def unit(s, delay):
        u = jnp.clip(s - delay, 0, n_units - 1)
        return ti_ref[u], tj_ref[u], tm_ref[u], tf_ref[u]

    def step(s, carry):
        i_c, j_c, _, f_c = unit(s, 2)
        va = va_ref[pl.ds(pl.multiple_of(j_c * tk, tk), tk), :]
        keep = jnp.where((f_c & 1) == 1, 0.0, 1.0)
        for r in range(NSA_GROUP):
            acc_ref[r] = acc_ref[r] * (al_ref[r] * keep) + _dot(p_ref[r], va)
        _, _, _, f_b = unit(s, 1)
        first_b = (f_b & 1) == 1
        for r in range(NSA_GROUP):
            for c in range(tq // NSA_CHUNK):
                rows = slice(c * NSA_CHUNK, (c + 1) * NSA_CHUNK)
                sc = s_ref[r, rows, :]
                m_prev = jnp.where(first_b, NEG, m_ref[r, rows, :])
                m_new = jnp.maximum(m_prev, jnp.max(sc, axis=1, keepdims=True))
                al_ref[r, rows, :] = jnp.exp2(m_prev - m_new)
                m_ref[r, rows, :] = m_new
                p_ref[r, rows, :] = jnp.exp2(sc - jnp.concatenate([m_new, m_new], axis=1)).astype(BF16)
        i_a, j_a, t_a, _ = unit(s, 0)
        q_rows = pl.ds(pl.multiple_of(i_a * tq, tq), tq)
        bias = bias_ref[t_a]
        if mode == "sel":
            chosen = _dot(sel_ref[q_rows, :], ex_ref[j_a])
            bias = bias + (chosen - 1.0) * MASK_BIG
        k = kv_ref[pl.ds(pl.multiple_of(j_a * tk, tk), tk), :][:, 0:d]
        for r in range(NSA_GROUP):
            s_ref[r] = lax.dot_general(qh_ref[r, q_rows, :], k, _NT, preferred_element_type=F32) + bias

        @pl.when(jnp.logical_and((f_c & 2) == 2, s >= NSA_STAGES - 1))
        def _():
            o_rows = pl.ds(pl.multiple_of(i_c * tq, tq), tq)
            gate = jax.nn.sigmoid(gt_ref[o_rows, :])
            outs = []
            for r in range(NSA_GROUP):
                acc = acc_ref[r]
                outs.append(acc[:, 0:d] / acc[:, d:d + 1] * gate[:, 3 * r + branch:3 * r + branch + 1])
            o_ref[o_rows, :] = jnp.concatenate(outs, axis=1)

        return carry

    lax.fori_loop(0, n_units + NSA_STAGES - 1, step, 0)


def _nsa_attn(q, kv, sel, gates, batch, seq, mode, tq=256):
    n = q.shape[0]
    nt = seq // tq
    g = NSA_KV_HEADS
    wq = NSA_GROUP * HEAD_DIM
    branch = 1 if mode == "sel" else 2
    assert WINDOW == 2 * tq
    if mode == "sel":
        units = [(i, j, int(j == i), int(j == 0) | 2 * int(j == i)) for i in range(nt) for j in range(i + 1)]
    else:
        units = [(i, j, (1, 0, 2)[i - j], int(j == max(i - 2, 0)) | 2 * int(j == i))
                 for i in range(nt) for j in range(max(i - 2, 0), i + 1)]
    tables = [jnp.array([u[c] for u in units], jnp.int32) for c in range(4)]
    kern = functools.partial(_nsa_attn_kernel, mode=mode, branch=branch, tq=tq, n_units=len(units))
    in_specs = [
        pl.BlockSpec((seq, wq), lambda b, h, *_: (b, h)),
        pl.BlockSpec((seq, 2 * HEAD_DIM), lambda b, h, *_: (b, h * 3 + branch)),
    ]
    args = [q, kv]
    scratch = [
        pltpu.VMEM((NSA_GROUP, seq, HEAD_DIM), BF16),
        pltpu.VMEM((seq, 2 * HEAD_DIM), BF16),
    ]
    if mode == "sel":
        in_specs.append(pl.BlockSpec((seq, LANES), lambda b, h, *_: (b * g + h, 0)))
        args.append(sel)
        scratch.append(pltpu.VMEM((nt, LANES, tq), BF16))
    in_specs.append(pl.BlockSpec((seq, LANES), lambda b, h, *_: (b, h)))
    args.append(gates)
    scratch += [
        pltpu.VMEM((3, tq, tq), F32),
        pltpu.VMEM((NSA_GROUP, tq, tq), F32),
        pltpu.VMEM((NSA_GROUP, tq, tq), BF16),
        pltpu.VMEM((NSA_GROUP, tq, 2 * HEAD_DIM), F32),
        pltpu.VMEM((NSA_GROUP, tq, 2 * HEAD_DIM), F32),
        pltpu.VMEM((NSA_GROUP, tq, 2 * HEAD_DIM), F32),
    ]
    grid_spec = pltpu.PrefetchScalarGridSpec(
        num_scalar_prefetch=4,
        grid=(batch, g),
        in_specs=in_specs,
        out_specs=pl.BlockSpec((seq, wq), lambda b, h, *_: (b, h)),
        scratch_shapes=scratch,
    )
    return pl.pallas_call(
        kern,
        grid_spec=grid_spec,
        out_shape=jax.ShapeDtypeStruct((n, g * wq), F32),
        compiler_params=_params("parallel", "parallel"),
        name="nsa_" + mode,
    )(*tables, *args)


SB_STAGES = 4
SB_CHUNK = 32


def _sb_kernel(ti_ref, tj_ref, q_ref, k_ref, v_ref, o_ref, qh_ref, kh_ref, later_ref, bias_ref,
               z_ref, hl_ref, zl_ref, a_ref, acc_ref, carry_ref, *, tq, n_units):
    tk = tq
    d = HEAD_DIM
    seq = q_ref.shape[0]
    row = lax.broadcasted_iota(jnp.int32, (tq, tk), 0)
    col = lax.broadcasted_iota(jnp.int32, (tq, tk), 1)
    later = jnp.where(row > col, 1.0, 0.0).astype(BF16)
    later_ref[...] = later
    bias_ref[0] = jnp.zeros((tq, tk), F32)
    bias_ref[1] = jnp.where(col < row, 0.0, NEG)
    z_ref[...] = jnp.zeros_like(z_ref)
    hl_ref[...] = jnp.zeros_like(hl_ref)
    zl_ref[...] = jnp.zeros_like(zl_ref)
    a_ref[...] = jnp.zeros_like(a_ref)
    acc_ref[...] = jnp.zeros_like(acc_ref)
    carry_ref[...] = jnp.zeros_like(carry_ref)

    def split(c, carry):
        rows = pl.ds(pl.multiple_of(c * tq, tq), tq)
        q = q_ref[rows, :]
        k = k_ref[rows, :]
        for h in range(2):
            qh_ref[h, rows, :] = q[:, h * d:(h + 1) * d]
            kh_ref[h, rows, :] = k[:, h * d:(h + 1) * d]
        return carry

    lax.fori_loop(0, seq // tq, split, 0)
    lane = lax.broadcasted_iota(jnp.int32, (tq, 2 * d), 1)

    def unit(s, delay):
        u = jnp.clip(s - delay, 0, n_units - 1)
        return ti_ref[u], tj_ref[u]

    def step(s, carry):
        i_e, j_e = unit(s, 3)
        v = v_ref[pl.ds(pl.multiple_of(j_e * tk, tk), tk), :]
        keep = jnp.where(i_e == j_e, 0.0, 1.0)
        accs = []
        for h in range(2):
            acc = acc_ref[h] * keep + _dot(a_ref[h], v)
            acc_ref[h] = acc
            accs.append(acc)
        o_ref[pl.ds(pl.multiple_of(i_e * tq, tq), tq), :] = jnp.where(lane < d, accs[0], accs[1]).astype(o_ref.dtype)
        for h in range(2):
            between = _dot(hl_ref[h], later_ref[...])
            for c in range(tq // SB_CHUNK):
                rows = slice(c * SB_CHUNK, (c + 1) * SB_CHUNK)
                a_ref[h, rows, :] = jnp.exp2(zl_ref[h, rows, :] + between[rows]).astype(BF16)
        i_b, j_b = unit(s, 1)
        keep_b = jnp.where(i_b == j_b, 0.0, 1.0)
        for h in range(2):
            for c in range(tq // SB_CHUNK):
                rows = slice(c * SB_CHUNK, (c + 1) * SB_CHUNK)
                z = z_ref[h, rows, :]
                nz = -z
                log_stay = jnp.minimum(nz, 0.0) - jnp.log(1.0 + jnp.exp2(jnp.minimum(z, nz))) * LOG2_E
                hl_ref[h, rows, :] = log_stay.astype(BF16)
                before = carry_ref[h, rows, :] * keep_b
                zl_ref[h, rows, :] = z + log_stay + before
                carry_ref[h, rows, :] = before + jnp.sum(log_stay, axis=1, keepdims=True)
        i_a, j_a = unit(s, 0)
        bias = bias_ref[jnp.where(i_a == j_a, 1, 0)]
        q_rows = pl.ds(pl.multiple_of(i_a * tq, tq), tq)
        k_rows = pl.ds(pl.multiple_of(j_a * tk, tk), tk)
        for h in range(2):
            z_ref[h] = lax.dot_general(qh_ref[h, q_rows, :], kh_ref[h, k_rows, :], _NT,
                                       preferred_element_type=F32) + bias
        return carry

    lax.fori_loop(0, n_units + SB_STAGES - 1, step, 0, unroll=2)


def _stick_breaking(qkv, batch, seq, heads, tq=256):
    n = qkv.shape[0]
    nt = seq // tq
    pairs = heads // 2
    units = [(i, j) for i in range(nt) for j in range(i, -1, -1)]
    ti = jnp.array([u[0] for u in units], jnp.int32)
    tj = jnp.array([u[1] for u in units], jnp.int32)
    kern = functools.partial(_sb_kernel, tq=tq, n_units=len(units))
    grid_spec = pltpu.PrefetchScalarGridSpec(
        num_scalar_prefetch=2,
        grid=(batch, pairs),
        in_specs=[
            pl.BlockSpec((seq, LANES), lambda b, h, ti, tj: (b, h)),
            pl.BlockSpec((seq, LANES), lambda b, h, ti, tj: (b, pairs + h)),
            pl.BlockSpec((seq, LANES), lambda b, h, ti, tj: (b, 2 * pairs + h)),
        ],
        out_specs=pl.BlockSpec((seq, LANES), lambda b, h, ti, tj: (b, h)),
        scratch_shapes=[
            pltpu.VMEM((2, seq, HEAD_DIM), BF16),
            pltpu.VMEM((2, seq, HEAD_DIM), BF16),
            pltpu.VMEM((tq, tq), BF16),
            pltpu.VMEM((2, tq, tq), F32),
            pltpu.VMEM((2, tq, tq), F32),
            pltpu.VMEM((2, tq, tq), BF16),
            pltpu.VMEM((2, tq, tq), F32),
            pltpu.VMEM((2, tq, tq), BF16),
            pltpu.VMEM((2, tq, LANES), F32),
            pltpu.VMEM((2, tq, 1), F32),
        ],
    )
    return pl.pallas_call(
        kern,
        grid_spec=grid_spec,
        out_shape=jax.ShapeDtypeStruct((n, heads * HEAD_DIM), BF16),
        compiler_params=_params("parallel", "parallel"),
        name="stick_breaking",
    )(ti, tj, qkv, qkv, qkv)


def _out_even_kernel(x_ref, yc_ref, oc_ref, os_ref, ow_ref, w_ref, o_ref):
    cd = yc_ref.shape[1]
    y_nsa = (oc_ref[...] + os_ref[...] + ow_ref[...]).astype(BF16)
    o_ref[...] = x_ref[...] + _dot(yc_ref[...], w_ref[0:cd, :]) + _dot(y_nsa, w_ref[cd:, :])


def _out_odd_kernel(x_ref, y_ref, w_ref, o_ref):
    o_ref[...] = x_ref[...] + _dot(y_ref[...], w_ref[...])


def _out_proj(kern, x, ys, w, tm=512):
    n, d = x.shape
    return pl.pallas_call(
        kern,
        grid=(n // tm,),
        in_specs=[pl.BlockSpec((tm, d), lambda i: (i, 0))]
        + [pl.BlockSpec((tm, y.shape[1]), lambda i: (i, 0)) for y in ys]
        + [pl.BlockSpec(w.shape, lambda i: (0, 0))],
        out_specs=pl.BlockSpec((tm, d), lambda i: (i, 0)),
        out_shape=jax.ShapeDtypeStruct((n, d), F32),
        compiler_params=_params("parallel"),
        name="out_proj",
    )(x, *ys, w)


def _final_norm_kernel(x_ref, g_ref, o_ref):
    o_ref[...] = _rms(x_ref[...], g_ref[...])


def _final_norm(x, g, tm=512):
    n, d = x.shape
    return pl.pallas_call(
        _final_norm_kernel,
        grid=(n // tm,),
        in_specs=[pl.BlockSpec((tm, d), lambda i: (i, 0)), pl.BlockSpec((1, d), lambda i: (0, 0))],
        out_specs=pl.BlockSpec((tm, d), lambda i: (i, 0)),
        out_shape=jax.ShapeDtypeStruct((n, d), F32),
        compiler_params=_params("parallel"),
        name="final_norm",
    )(x, g.reshape(1, d))


def _overlap_matrix(nch, nblk):
    c0 = jnp.arange(nch) * CMP_STRIDE
    s0 = jnp.arange(LANES) * SEL_BLOCK
    lo = jnp.maximum(c0[:, None], s0[None, :])
    hi = jnp.minimum(c0[:, None] + CMP_BLOCK, s0[None, :] + SEL_BLOCK)
    ovl = jnp.maximum(hi - lo, 0).astype(F32) / CMP_BLOCK
    keep = (jnp.arange(nch)[:, None] < nch - 1) & (jnp.arange(LANES)[None, :] < nblk)
    return jnp.where(keep, ovl, 0.0).astype(BF16)


def _conv_nsa_mixer(x, g, w_in, conv_w, pe_k, w1_k, w2_k, pe_v, w1_v, w2_v, w_out, batch, seq):
    d = HEAD_DIM
    cd = conv_w.shape[1]
    kvd = NSA_KV_HEADS * d
    o_q = 3 * cd
    o_kv = o_q + NSA_KV_HEADS * NSA_GROUP * d
    o_g = o_kv + 6 * kvd
    ng = NSA_GROUP * 3
    kv_cols = []
    gate_cols = []
    for h in range(NSA_KV_HEADS):
        for typ in range(3):
            base = o_kv + typ * 2 * kvd + h * d
            kv_cols += [w_in[:, base:base + d], w_in[:, base + kvd:base + kvd + d]]
        gate_cols += [w_in[:, o_g + h * ng:o_g + (h + 1) * ng], jnp.zeros((w_in.shape[0], LANES - ng), F32)]
    w_all = jnp.concatenate(
        [w_in[:, :o_q], w_in[:, o_q:o_kv] * (Q_SCALE * LOG2_E)] + kv_cols + gate_cols, axis=1).astype(BF16)
    cv, q, kv, gates = _norm_proj(
        x, g, w_all,
        [(3 * cd, F32), (o_kv - o_q, BF16), (6 * kvd, BF16), (NSA_KV_HEADS * LANES, F32)])

    y_conv = _short_conv(cv, conv_w, seq)

    nch = seq // CMP_STRIDE
    kv6 = kv.reshape(batch, seq, NSA_KV_HEADS, 3, 2, d)
    cm = kv6[:, :, :, 0].transpose(0, 2, 3, 1, 4).reshape(batch * NSA_KV_HEADS * 2 * nch, CMP_STRIDE * d)
    w1 = jnp.stack([w1_k, w1_v]).astype(BF16)
    w2 = jnp.stack([w2_k, w2_v]).astype(BF16)
    pe = jnp.stack([pe_k.reshape(1, -1), pe_v.reshape(1, -1)])
    kvc = _compress(cm, w1, w2, pe, nch)

    ovl = _overlap_matrix(nch, seq // SEL_BLOCK)
    o_cmp, sel = _cmp_branch(q, kvc, ovl, gates, batch, seq)
    o_sel = _nsa_attn(q, kv, sel, gates, batch, seq, "sel")
    o_win = _nsa_attn(q, kv, sel, gates, batch, seq, "win")
    return _out_proj(_out_even_kernel, x, [y_conv, o_cmp, o_sel, o_win], w_out.astype(BF16))


def _stick_breaking_mixer(x, g, w_qkv, w_out, batch, seq):
    hd = w_out.shape[0]
    w = jnp.concatenate([w_qkv[:, :hd] * (Q_SCALE * LOG2_E), w_qkv[:, hd:]], axis=1).astype(BF16)
    (qkv,) = _norm_proj(x, g, w, [(3 * hd, BF16)])
    y = _stick_breaking(qkv, batch, seq, hd // HEAD_DIM)
    return _out_proj(_out_odd_kernel, x, [y], w_out.astype(BF16))


def kernel(x, norm_ffn1, w_ffn1_in, w_ffn1_out, norm_mix, w_in_ab, conv_w, cmp_pe_k, cmp_w1_k, cmp_w2_k,
           cmp_pe_v, cmp_w1_v, cmp_w2_v, w_out_ab, w_qkv_sb, w_out_sb, norm_ffn2, w_ffn2_in, w_ffn2_out,
           norm_final):
    batch, seq, d_model = x.shape
    depth = norm_ffn1.shape[0]
    x = x.reshape(batch * seq, d_model)
    for layer in range(depth):
        x = _ffn(x, norm_ffn1[layer], w_ffn1_in[layer].astype(BF16), w_ffn1_out[layer].astype(BF16))
        i = layer // 2
        if layer % 2 == 0:
            x = _conv_nsa_mixer(x, norm_mix[layer], w_in_ab[i], conv_w[i], cmp_pe_k[i], cmp_w1_k[i],
                                cmp_w2_k[i], cmp_pe_v[i], cmp_w1_v[i], cmp_w2_v[i], w_out_ab[i], batch, seq)
        else:
            x = _stick_breaking_mixer(x, norm_mix[layer], w_qkv_sb[i], w_out_sb[i], batch, seq)
        x = _ffn(x, norm_ffn2[layer], w_ffn2_in[layer].astype(BF16), w_ffn2_out[layer].astype(BF16))
    return _final_norm(x, norm_final).reshape(batch, seq, d_model)
```

```python
import functools

import jax
import jax.numpy as jnp
from jax import lax
from jax.experimental import pallas as pl
from jax.experimental.pallas import tpu as pltpu

F32 = jnp.float32
BF16 = jnp.bfloat16

EPS = 1e-6
NEG = -1e30
HEAD_DIM = 64
Q_SCALE = HEAD_DIM ** -0.5
LOG2_E = 1.4426950408889634
CONV_WIDTH = 3
NSA_KV_HEADS = 2
NSA_GROUP = 4
CMP_BLOCK = 32
CMP_STRIDE = 16
SEL_BLOCK = 64
SEL_TOPK = 16
WINDOW = 512
FORCE_BONUS = 1e4
LANES = 128
VMEM_LIMIT = 56 * 1024 * 1024

_NT = (((1,), (1,)), ((), ()))


def _params(*sem, flags=None):
    return pltpu.CompilerParams(dimension_semantics=sem, vmem_limit_bytes=VMEM_LIMIT, flags=flags)


def _rms(x, g):
    ms = jnp.mean(x * x, axis=-1, keepdims=True)
    return x * lax.rsqrt(ms + EPS) * g


def _dot(a, b):
    return jnp.dot(a, b, preferred_element_type=F32)


def _ffn_kernel(x_ref, g_ref, wi_ref, wo_ref, o_ref, *, tf):
    f = wo_ref.shape[0]
    x = x_ref[...]
    h = _rms(x, g_ref[...]).astype(BF16)
    acc = None
    for c in range(f // tf):
        gate = _dot(h, wi_ref[:, c * tf:(c + 1) * tf])
        up = _dot(h, wi_ref[:, f + c * tf:f + (c + 1) * tf])
        act = gate * jax.nn.sigmoid(gate) * up
        part = _dot(act.astype(BF16), wo_ref[c * tf:(c + 1) * tf, :])
        acc = part if acc is None else acc + part
    o_ref[...] = x + 0.5 * acc


def _ffn(x, g, w_in, w_out, tm=512):
    n, d = x.shape
    f = w_out.shape[0]
    tf = f // 2 if (f // 2) % LANES == 0 else f
    resident = pl.Buffered(1)
    return pl.pallas_call(
        functools.partial(_ffn_kernel, tf=tf),
        grid=(n // tm,),
        in_specs=[
            pl.BlockSpec((tm, d), lambda i: (i, 0)),
            pl.BlockSpec((1, d), lambda i: (0, 0)),
            pl.BlockSpec(w_in.shape, lambda i: (0, 0), pipeline_mode=resident),
            pl.BlockSpec(w_out.shape, lambda i: (0, 0), pipeline_mode=resident),
        ],
        out_specs=pl.BlockSpec((tm, d), lambda i: (i, 0)),
        out_shape=jax.ShapeDtypeStruct((n, d), F32),
        compiler_params=_params("parallel"),
        name="ffn",
    )(x, g.reshape(1, d), w_in, w_out)


def _proj_kernel(x_ref, g_ref, w_ref, *o_refs):
    h = _rms(x_ref[...], g_ref[...]).astype(BF16)
    off = 0
    for o_ref in o_refs:
        width = o_ref.shape[1]
        o_ref[...] = _dot(h, w_ref[:, off:off + width]).astype(o_ref.dtype)
        off += width


def _norm_proj(x, g, w, outs, tm=512):
    n, d = x.shape
    return pl.pallas_call(
        _proj_kernel,
        grid=(n // tm,),
        in_specs=[
            pl.BlockSpec((tm, d), lambda i: (i, 0)),
            pl.BlockSpec((1, d), lambda i: (0, 0)),
            pl.BlockSpec(w.shape, lambda i: (0, 0)),
        ],
        out_specs=[pl.BlockSpec((tm, wd), lambda i: (i, 0)) for wd, _ in outs],
        out_shape=[jax.ShapeDtypeStruct((n, wd), dt) for wd, dt in outs],
        compiler_params=_params("parallel"),
        name="norm_proj",
    )(x, g.reshape(1, d), w)


def _conv_kernel(cv_ref, prev_ref, w_ref, o_ref, ext_ref, *, tiles_per_seq, cd):
    tm = o_ref.shape[0]
    first = (pl.program_id(0) % tiles_per_seq) == 0
    u = cv_ref[:, cd:2 * cd] * cv_ref[:, 2 * cd:3 * cd]
    u_prev = prev_ref[:, cd:2 * cd] * prev_ref[:, 2 * cd:3 * cd]
    ext_ref[0:8, :] = jnp.where(first, 0.0, u_prev)
    ext_ref[8:, :] = u
    w = w_ref[...]
    y = w[0:1] * ext_ref[6:6 + tm, :] + w[1:2] * ext_ref[7:7 + tm, :] + w[2:3] * u
    o_ref[...] = (cv_ref[:, 0:cd] * y).astype(o_ref.dtype)


def _short_conv(cv, conv_w, seq, tm=512):
    n = cv.shape[0]
    cd = conv_w.shape[1]
    kern = functools.partial(_conv_kernel, tiles_per_seq=seq // tm, cd=cd)
    return pl.pallas_call(
        kern,
        grid=(n // tm,),
        in_specs=[
            pl.BlockSpec((tm, 3 * cd), lambda i: (i, 0)),
            pl.BlockSpec((8, 3 * cd), lambda i: (jnp.maximum(i * (tm // 8) - 1, 0), 0)),
            pl.BlockSpec(conv_w.shape, lambda i: (0, 0)),
        ],
        out_specs=pl.BlockSpec((tm, cd), lambda i: (i, 0)),
        out_shape=jax.ShapeDtypeStruct((n, cd), BF16),
        scratch_shapes=[pltpu.VMEM((tm + 8, cd), F32)],
        compiler_params=_params("parallel"),
        name="short_conv",
    )(cv, cv, conv_w)


def _compress_kernel(cm_ref, w1_ref, w2_ref, pe_ref, o_ref):
    nch = o_ref.shape[0]
    half = cm_ref.shape[1]
    outs = []
    for t in range(2):
        cm = cm_ref[t * nch:(t + 1) * nch, :]
        a = _dot(cm, w1_ref[t, 0:half, :])
        b = _dot(cm, w1_ref[t, half:2 * half, :])
        pe = jnp.broadcast_to(pe_ref[t], (8, 2 * half)).astype(BF16)
        bias = _dot(pe, w1_ref[t])[0:1]
        hid = a + pltpu.roll(b, nch - 1, 0) + bias
        outs.append(_dot(jax.nn.gelu(hid).astype(BF16), w2_ref[t]))
    o_ref[...] = jnp.concatenate(outs, axis=1).astype(o_ref.dtype)


def _compress(cm, w1, w2, pe, nch):
    rows, half = cm.shape
    bg = rows // (2 * nch)
    d = w2.shape[2]
    return pl.pallas_call(
        _compress_kernel,
        grid=(bg,),
        in_specs=[
            pl.BlockSpec((2 * nch, half), lambda i: (i, 0)),
            pl.BlockSpec(w1.shape, lambda i: (0, 0, 0)),
            pl.BlockSpec(w2.shape, lambda i: (0, 0, 0)),
            pl.BlockSpec(pe.shape, lambda i: (0, 0, 0)),
        ],
        out_specs=pl.BlockSpec((nch, 2 * d), lambda i: (i, 0)),
        out_shape=jax.ShapeDtypeStruct((bg * nch, 2 * d), BF16),
        compiler_params=_params("parallel"),
        name="compress",
    )(cm, w1, w2, pe)


def _cmp_kernel(q_ref, kvc_ref, ovl_ref, gt_ref, o_ref, sel_ref, *, nblk):
    tq = q_ref.shape[0]
    nc = kvc_ref.shape[0]
    d = HEAD_DIM
    i = pl.program_id(2)
    q = q_ref[...]
    kvc = kvc_ref[...]
    kc = kvc[:, 0:d]
    pos = i * tq + lax.broadcasted_iota(jnp.int32, (tq, nc), 0)
    cidx = lax.broadcasted_iota(jnp.int32, (tq, nc), 1)
    cmask = cidx * CMP_STRIDE + (CMP_BLOCK - 1) <= pos
    gate = jax.nn.sigmoid(gt_ref[...])
    imp = jnp.zeros((tq, LANES), F32)
    outs = []
    for r in range(NSA_GROUP):
        s = lax.dot_general(q[:, r * d:(r + 1) * d], kc, _NT, preferred_element_type=F32)
        s = jnp.where(cmask, s, NEG)
        e = jnp.exp2(s - jnp.max(s, axis=1, keepdims=True))
        p = jnp.where(cmask, e / jnp.sum(e, axis=1, keepdims=True), 0.0)
        pb = p.astype(BF16)
        o = _dot(pb, kvc)[:, d:2 * d]
        outs.append(o * gate[:, 3 * r:3 * r + 1])
        imp = imp + _dot(pb, ovl_ref[...])
    o_ref[...] = jnp.concatenate(outs, axis=1)

    posb = i * tq + lax.broadcasted_iota(jnp.int32, (tq, LANES), 0)
    blk = lax.broadcasted_iota(jnp.int32, (tq, LANES), 1)
    cur = posb >> 6
    forced = jnp.where(blk == 0, 1.0, jnp.where(blk == cur, 1.0, jnp.where(blk == cur - 1, 1.0, 0.0)))
    score = jnp.where(blk * SEL_BLOCK <= posb, imp + forced * FORCE_BONUS, NEG)
    sc = score.T[0:nblk, :]
    grp = 8
    sub = lax.broadcasted_iota(jnp.int32, (grp, tq), 0)
    groups = [sc[g0:g0 + grp, :] for g0 in range(0, nblk, grp)]
    ranks = [jnp.zeros((grp, tq), F32) for _ in groups]
    for sp in range(nblk):
        other = sc[sp:sp + 1, :]
        for gi, mine in enumerate(groups):
            g0 = gi * grp
            ge = jnp.where(other >= mine, 1.0, 0.0)
            gt = jnp.where(other > mine, 1.0, 0.0)
            if g0 > sp:
                beats = ge
            elif g0 + grp - 1 < sp:
                beats = gt
            else:
                beats = jnp.where(sub + g0 > sp, ge, gt)
            ranks[gi] = ranks[gi] + beats
    chosen = jnp.where(jnp.concatenate(ranks, axis=0) < SEL_TOPK, 1.0, 0.0)
    chosen = jnp.concatenate([chosen, jnp.zeros((LANES - nblk, tq), F32)], axis=0)
    sel_ref[...] = chosen.T.astype(sel_ref.dtype)


def _cmp_branch(q, kvc, ovl, gates, batch, seq, tq=512):
    n = q.shape[0]
    nt = seq // tq
    g = NSA_KV_HEADS
    wq = NSA_GROUP * HEAD_DIM
    nc = kvc.shape[0] // (batch * g)
    kern = functools.partial(_cmp_kernel, nblk=seq // SEL_BLOCK)
    return pl.pallas_call(
        kern,
        grid=(batch, g, nt),
        in_specs=[
            pl.BlockSpec((tq, wq), lambda b, h, i: (b * nt + i, h)),
            pl.BlockSpec((nc, 2 * HEAD_DIM), lambda b, h, i: (b * g + h, 0)),
            pl.BlockSpec(ovl.shape, lambda b, h, i: (0, 0)),
            pl.BlockSpec((tq, LANES), lambda b, h, i: (b * nt + i, h)),
        ],
        out_specs=[
            pl.BlockSpec((tq, wq), lambda b, h, i: (b * nt + i, h)),
            pl.BlockSpec((tq, LANES), lambda b, h, i: ((b * g + h) * nt + i, 0)),
        ],
        out_shape=[
            jax.ShapeDtypeStruct((n, g * wq), F32),
            jax.ShapeDtypeStruct((batch * g * seq, LANES), BF16),
        ],
        compiler_params=_params("parallel", "parallel", "parallel"),
        name="nsa_cmp_select",
    )(q, kvc, ovl, gates)


NSA_STAGES = 3
NSA_CHUNK = 32
MASK_BIG = 1e30


def _nsa_attn_kernel(ti_ref, tj_ref, tm_ref, tf_ref, q_ref, kv_ref, *rest, mode, branch, tq, n_units):
    if mode == "sel":
        (sel_ref, gt_ref, o_ref, qh_ref, kk_ref, va_ref, ex_ref, bias_ref, s_ref, p_ref, al_ref, m_ref, acc_ref,
         fin_ref) = rest
    else:
        gt_ref, o_ref, qh_ref, kk_ref, va_ref, bias_ref, s_ref, p_ref, al_ref, m_ref, acc_ref, fin_ref = rest
    tk = tq
    d = HEAD_DIM
    seq = q_ref.shape[0]
    row = lax.broadcasted_iota(jnp.int32, (tq, tk), 0)
    col = lax.broadcasted_iota(jnp.int32, (tq, tk), 1)
    bias_ref[0] = jnp.zeros((tq, tk), F32)
    bias_ref[1] = jnp.where(col <= row, 0.0, NEG)
    bias_ref[2] = jnp.where(col > row, 0.0, NEG)
    s_ref[...] = jnp.zeros_like(s_ref)
    p_ref[...] = jnp.zeros_like(p_ref)
    al_ref[...] = jnp.zeros_like(al_ref)
    acc_ref[...] = jnp.zeros_like(acc_ref)
    m_ref[...] = jnp.full(m_ref.shape, NEG, F32)
    lane = lax.broadcasted_iota(jnp.int32, (tq, 2 * d), 1)
    srow = lax.broadcasted_iota(jnp.int32, (LANES, tk), 0)
    scol = lax.broadcasted_iota(jnp.int32, (LANES, tk), 1) >> 6

    def prep(c, carry):
        rows = pl.ds(pl.multiple_of(c * tq, tq), tq)
        q = q_ref[rows, :]
        for r in range(NSA_GROUP):
            pair = q[:, (r // 2) * 2 * d:(r // 2 + 1) * 2 * d]
            mine = (lane < d) if r % 2 == 0 else (lane >= d)
            qh_ref[r, rows, :] = jnp.where(mine, pair, 0.0).astype(BF16)
        kv = kv_ref[rows, :]
        k_first = kv[:, 0:d]
        kk_ref[rows, :] = jnp.concatenate([k_first, k_first], axis=1)
        v_first = jnp.concatenate([kv[:, d:2 * d], k_first], axis=1)
        va_ref[rows, :] = jnp.where(lane < d, v_first, 1.0).astype(BF16)
        if mode == "sel":
            ex_ref[c] = jnp.where(srow == c * (tk // SEL_BLOCK) + scol, 1.0, 0.0).astype(BF16)
        return carry

    lax.fori_loop(0, seq // tq, prep, 0)

    def unit(s, delay):
        u = jnp.clip(s - delay, 0, n_units - 1)
        return ti_ref[u], tj_ref[u], tm_ref[u], tf_ref[u]

    def step(s, carry):
        i_c, j_c, _, f_c = unit(s, 2)
        va = va_ref[pl.ds(pl.multiple_of(j_c * tk, tk), tk), :]
        keep = jnp.where((f_c & 1) == 1, 0.0, 1.0)
        for r in range(NSA_GROUP):
            acc = acc_ref[r] * (al_ref[r] * keep) + _dot(p_ref[r], va)
            acc_ref[r] = acc
            fin_ref[i_c, r] = acc
        _, _, _, f_b = unit(s, 1)
        first_b = (f_b & 1) == 1
        for r in range(NSA_GROUP):
            for c in range(tq // NSA_CHUNK):
                rows = slice(c * NSA_CHUNK, (c + 1) * NSA_CHUNK)
                sc = s_ref[r, rows, :]
                m_prev = jnp.where(first_b, NEG, m_ref[r, rows, :])
                m_new = jnp.maximum(m_prev, jnp.max(sc, axis=1, keepdims=True))
                al_ref[r, rows, :] = jnp.exp2(m_prev - m_new)
                m_ref[r, rows, :] = m_new
                p_ref[r, rows, :] = jnp.exp2(sc - jnp.concatenate([m_new, m_new], axis=1)).astype(BF16)
        i_a, j_a, t_a, _ = unit(s, 0)
        q_rows = pl.ds(pl.multiple_of(i_a * tq, tq), tq)
        if mode == "sel":
            chosen = _dot(sel_ref[q_rows, :], ex_ref[j_a])
            bias_ref[3] = bias_ref[t_a] + (chosen - 1.0) * MASK_BIG
            t_a = 3
        k = kk_ref[pl.ds(pl.multiple_of(j_a * tk, tk), tk), :]
        for r in range(NSA_GROUP):
            s_ref[r] = (lax.dot_general(qh_ref[r, q_rows, :], k, _NT, preferred_element_type=F32)
                        + bias_ref[t_a])

        return carry

    lax.fori_loop(0, n_units + NSA_STAGES - 1, step, 0, unroll=2 if mode == "win" else 1)

    def finish(c, carry):
        o_rows = pl.ds(pl.multiple_of(c * tq, tq), tq)
        gate = jax.nn.sigmoid(gt_ref[o_rows, :])
        outs = []
        for r in range(NSA_GROUP):
            acc = fin_ref[c, r]
            outs.append(acc[:, 0:d] / acc[:, d:d + 1] * gate[:, 3 * r + branch:3 * r + branch + 1])
        o_ref[o_rows, :] = jnp.concatenate(outs, axis=1)
        return carry

    lax.fori_loop(0, seq // tq, finish, 0)


def _nsa_attn(q, kv, sel, gates, batch, seq, mode, tq=256):
    n = q.shape[0]
    nt = seq // tq
    g = NSA_KV_HEADS
    wq = NSA_GROUP * HEAD_DIM
    branch = 1 if mode == "sel" else 2
    assert WINDOW == 2 * tq
    if mode == "sel":
        units = [(i, j, int(j == i), int(j == 0) | 2 * int(j == i)) for i in range(nt) for j in range(i + 1)]
    else:
        units = [(i, j, (1, 0, 2)[i - j], int(j == max(i - 2, 0)) | 2 * int(j == i))
                 for i in range(nt) for j in range(max(i - 2, 0), i + 1)]
    tables = [jnp.array([u[c] for u in units], jnp.int32) for c in range(4)]
    kern = functools.partial(_nsa_attn_kernel, mode=mode, branch=branch, tq=tq, n_units=len(units))
    in_specs = [
        pl.BlockSpec((seq, wq), lambda b, h, *_: (b, h)),
        pl.BlockSpec((seq, 2 * HEAD_DIM), lambda b, h, *_: (b, h * 3 + branch)),
    ]
    args = [q, kv]
    scratch = [
        pltpu.VMEM((NSA_GROUP, seq, 2 * HEAD_DIM), BF16),
        pltpu.VMEM((seq, 2 * HEAD_DIM), BF16),
        pltpu.VMEM((seq, 2 * HEAD_DIM), BF16),
    ]
    if mode == "sel":
        in_specs.append(pl.BlockSpec((seq, LANES), lambda b, h, *_: (b * g + h, 0)))
        args.append(sel)
        scratch.append(pltpu.VMEM((nt, LANES, tq), BF16))
    in_specs.append(pl.BlockSpec((seq, LANES), lambda b, h, *_: (b, h)))
    args.append(gates)
    scratch += [
        pltpu.VMEM((4, tq, tq), F32),
        pltpu.VMEM((NSA_GROUP, tq, tq), F32),
        pltpu.VMEM((NSA_GROUP, tq, tq), BF16),
        pltpu.VMEM((NSA_GROUP, tq, 2 * HEAD_DIM), F32),
        pltpu.VMEM((NSA_GROUP, tq, 2 * HEAD_DIM), F32),
        pltpu.VMEM((NSA_GROUP, tq, 2 * HEAD_DIM), F32),
        pltpu.VMEM((nt, NSA_GROUP, tq, 2 * HEAD_DIM), F32),
    ]
    grid_spec = pltpu.PrefetchScalarGridSpec(
        num_scalar_prefetch=4,
        grid=(batch, g),
        in_specs=in_specs,
        out_specs=pl.BlockSpec((seq, wq), lambda b, h, *_: (b, h)),
        scratch_shapes=scratch,
    )
    return pl.pallas_call(
        kern,
        grid_spec=grid_spec,
        out_shape=jax.ShapeDtypeStruct((n, g * wq), F32),
        compiler_params=_params("parallel", "parallel"),
        name="nsa_" + mode,
    )(*tables, *args)


SB_STAGES = 4
SB_CHUNK = 32


def _sb_kernel(ti_ref, tj_ref, q_ref, k_ref, v_ref, o_ref, qh_ref, later_ref, bias_ref,
               z_ref, hl_ref, zl_ref, a_ref, acc_ref, carry_ref, *, tq, n_units):
    tk = tq
    d = HEAD_DIM
    seq = q_ref.shape[0]
    row = lax.broadcasted_iota(jnp.int32, (tq, tk), 0)
    col = lax.broadcasted_iota(jnp.int32, (tq, tk), 1)
    later = jnp.where(row > col, 1.0, 0.0).astype(BF16)
    later_ref[...] = later
    bias_ref[0] = jnp.zeros((tq, tk), F32)
    bias_ref[1] = jnp.where(col < row, 0.0, NEG)
    z_ref[...] = jnp.zeros_like(z_ref)
    hl_ref[...] = jnp.zeros_like(hl_ref)
    zl_ref[...] = jnp.zeros_like(zl_ref)
    a_ref[...] = jnp.zeros_like(a_ref)
    acc_ref[...] = jnp.zeros_like(acc_ref)
    carry_ref[...] = jnp.zeros_like(carry_ref)

    lane = lax.broadcasted_iota(jnp.int32, (tq, 2 * d), 1)

    def split(c, carry):
        rows = pl.ds(pl.multiple_of(c * tq, tq), tq)
        q = q_ref[rows, :]
        qh_ref[0, rows, :] = jnp.where(lane < d, q, 0.0).astype(BF16)
        qh_ref[1, rows, :] = jnp.where(lane >= d, q, 0.0).astype(BF16)
        return carry

    lax.fori_loop(0, seq // tq, split, 0)

    def unit(s, delay):
        u = jnp.clip(s - delay, 0, n_units - 1)
        return ti_ref[u], tj_ref[u]

    def step(s, carry):
        i_e, j_e = unit(s, 3)
        v = v_ref[pl.ds(pl.multiple_of(j_e * tk, tk), tk), :]
        keep = jnp.where(i_e == j_e, 0.0, 1.0)
        accs = []
        for h in range(2):
            acc = acc_ref[h] * keep + _dot(a_ref[h], v)
            acc_ref[h] = acc
            accs.append(acc)
        o_ref[pl.ds(pl.multiple_of(i_e * tq, tq), tq), :] = jnp.where(lane < d, accs[0], accs[1]).astype(o_ref.dtype)
        for h in range(2):
            between = _dot(hl_ref[h], later_ref[...])
            for c in range(tq // SB_CHUNK):
                rows = slice(c * SB_CHUNK, (c + 1) * SB_CHUNK)
                a_ref[h, rows, :] = jnp.exp2(zl_ref[h, rows, :] + between[rows]).astype(BF16)
        i_b, j_b = unit(s, 1)
        keep_b = jnp.where(i_b == j_b, 0.0, 1.0)
        for h in range(2):
            for c in range(tq // SB_CHUNK):
                rows = slice(c * SB_CHUNK, (c + 1) * SB_CHUNK)
                z = z_ref[h, rows, :]
                nz = -z
                log_stay = jnp.minimum(nz, 0.0) - jnp.log(1.0 + jnp.exp2(jnp.minimum(z, nz))) * LOG2_E
                hl_ref[h, rows, :] = log_stay.astype(BF16)
                before = carry_ref[h, rows, :] * keep_b
                zl_ref[h, rows, :] = z + log_stay + before
                carry_ref[h, rows, :] = before + jnp.sum(log_stay, axis=1, keepdims=True)
        i_a, j_a = unit(s, 0)
        bias = bias_ref[jnp.where(i_a == j_a, 1, 0)]
        q_rows = pl.ds(pl.multiple_of(i_a * tq, tq), tq)
        k_rows = pl.ds(pl.multiple_of(j_a * tk, tk), tk)
        for h in range(2):
            z_ref[h] = lax.dot_general(qh_ref[h, q_rows, :], k_ref[k_rows, :], _NT,
                                       preferred_element_type=F32) + bias
        return carry

    lax.fori_loop(0, n_units + SB_STAGES - 1, step, 0, unroll=2)


def _stick_breaking(qkv, batch, seq, heads, tq=256):
    n = qkv.shape[0]
    nt = seq // tq
    pairs = heads // 2
    units = [(i, j) for i in range(nt) for j in range(i, -1, -1)]
    ti = jnp.array([u[0] for u in units], jnp.int32)
    tj = jnp.array([u[1] for u in units], jnp.int32)
    kern = functools.partial(_sb_kernel, tq=tq, n_units=len(units))
    grid_spec = pltpu.PrefetchScalarGridSpec(
        num_scalar_prefetch=2,
        grid=(batch, pairs),
        in_specs=[
            pl.BlockSpec((seq, LANES), lambda b, h, ti, tj: (b, h)),
            pl.BlockSpec((seq, LANES), lambda b, h, ti, tj: (b, pairs + h)),
            pl.BlockSpec((seq, LANES), lambda b, h, ti, tj: (b, 2 * pairs + h)),
        ],
        out_specs=pl.BlockSpec((seq, LANES), lambda b, h, ti, tj: (b, h)),
        scratch_shapes=[
            pltpu.VMEM((2, seq, 2 * HEAD_DIM), BF16),
            pltpu.VMEM((tq, tq), BF16),
            pltpu.VMEM((2, tq, tq), F32),
            pltpu.VMEM((2, tq, tq), F32),
            pltpu.VMEM((2, tq, tq), BF16),
            pltpu.VMEM((2, tq, tq), F32),
            pltpu.VMEM((2, tq, tq), BF16),
            pltpu.VMEM((2, tq, LANES), F32),
            pltpu.VMEM((2, tq, 1), F32),
        ],
    )
    return pl.pallas_call(
        kern,
        grid_spec=grid_spec,
        out_shape=jax.ShapeDtypeStruct((n, heads * HEAD_DIM), BF16),
        compiler_params=_params("parallel", "parallel"),
        name="stick_breaking",
    )(ti, tj, qkv, qkv, qkv)


def _out_even_kernel(x_ref, yc_ref, oc_ref, os_ref, ow_ref, w_ref, o_ref):
    cd = yc_ref.shape[1]
    y_nsa = (oc_ref[...] + os_ref[...] + ow_ref[...]).astype(BF16)
    o_ref[...] = x_ref[...] + _dot(yc_ref[...], w_ref[0:cd, :]) + _dot(y_nsa, w_ref[cd:, :])


def _out_odd_kernel(x_ref, y_ref, w_ref, o_ref):
    o_ref[...] = x_ref[...] + _dot(y_ref[...], w_ref[...])


def _out_proj(kern, x, ys, w, tm=512):
    n, d = x.shape
    return pl.pallas_call(
        kern,
        grid=(n // tm,),
        in_specs=[pl.BlockSpec((tm, d), lambda i: (i, 0))]
        + [pl.BlockSpec((tm, y.shape[1]), lambda i: (i, 0)) for y in ys]
        + [pl.BlockSpec(w.shape, lambda i: (0, 0))],
        out_specs=pl.BlockSpec((tm, d), lambda i: (i, 0)),
        out_shape=jax.ShapeDtypeStruct((n, d), F32),
        compiler_params=_params("parallel"),
        name="out_proj",
    )(x, *ys, w)


def _final_norm_kernel(x_ref, g_ref, o_ref):
    o_ref[...] = _rms(x_ref[...], g_ref[...])


def _final_norm(x, g, tm=512):
    n, d = x.shape
    return pl.pallas_call(
        _final_norm_kernel,
        grid=(n // tm,),
        in_specs=[pl.BlockSpec((tm, d), lambda i: (i, 0)), pl.BlockSpec((1, d), lambda i: (0, 0))],
        out_specs=pl.BlockSpec((tm, d), lambda i: (i, 0)),
        out_shape=jax.ShapeDtypeStruct((n, d), F32),
        compiler_params=_params("parallel"),
        name="final_norm",
    )(x, g.reshape(1, d))


def _overlap_matrix(nch, nblk):
    c0 = jnp.arange(nch) * CMP_STRIDE
    s0 = jnp.arange(LANES) * SEL_BLOCK
    lo = jnp.maximum(c0[:, None], s0[None, :])
    hi = jnp.minimum(c0[:, None] + CMP_BLOCK, s0[None, :] + SEL_BLOCK)
    ovl = jnp.maximum(hi - lo, 0).astype(F32) / CMP_BLOCK
    keep = (jnp.arange(nch)[:, None] < nch - 1) & (jnp.arange(LANES)[None, :] < nblk)
    return jnp.where(keep, ovl, 0.0).astype(BF16)


def _conv_nsa_mixer(x, g, w_in, conv_w, pe_k, w1_k, w2_k, pe_v, w1_v, w2_v, w_out, batch, seq):
    d = HEAD_DIM
    cd = conv_w.shape[1]
    kvd = NSA_KV_HEADS * d
    o_q = 3 * cd
    o_kv = o_q + NSA_KV_HEADS * NSA_GROUP * d
    o_g = o_kv + 6 * kvd
    ng = NSA_GROUP * 3
    kv_cols = []
    gate_cols = []
    for h in range(NSA_KV_HEADS):
        for typ in range(3):
            base = o_kv + typ * 2 * kvd + h * d
            kv_cols += [w_in[:, base:base + d], w_in[:, base + kvd:base + kvd + d]]
        gate_cols += [w_in[:, o_g + h * ng:o_g + (h + 1) * ng], jnp.zeros((w_in.shape[0], LANES - ng), F32)]
    w_all = jnp.concatenate(
        [w_in[:, :o_q], w_in[:, o_q:o_kv] * (Q_SCALE * LOG2_E)] + kv_cols + gate_cols, axis=1).astype(BF16)
    cv, q, kv, gates = _norm_proj(
        x, g, w_all,
        [(3 * cd, F32), (o_kv - o_q, BF16), (6 * kvd, BF16), (NSA_KV_HEADS * LANES, F32)])

    y_conv = _short_conv(cv, conv_w, seq)

    nch = seq // CMP_STRIDE
    kv6 = kv.reshape(batch, seq, NSA_KV_HEADS, 3, 2, d)
    cm = kv6[:, :, :, 0].transpose(0, 2, 3, 1, 4).reshape(batch * NSA_KV_HEADS * 2 * nch, CMP_STRIDE * d)
    w1 = jnp.stack([w1_k, w1_v]).astype(BF16)
    w2 = jnp.stack([w2_k, w2_v]).astype(BF16)
    pe = jnp.stack([pe_k.reshape(1, -1), pe_v.reshape(1, -1)])
    kvc = _compress(cm, w1, w2, pe, nch)

    ovl = _overlap_matrix(nch, seq // SEL_BLOCK)
    o_cmp, sel = _cmp_branch(q, kvc, ovl, gates, batch, seq)
    o_sel = _nsa_attn(q, kv, sel, gates, batch, seq, "sel")
    o_win = _nsa_attn(q, kv, sel, gates, batch, seq, "win")
    return _out_proj(_out_even_kernel, x, [y_conv, o_cmp, o_sel, o_win], w_out.astype(BF16))


def _stick_breaking_mixer(x, g, w_qkv, w_out, batch, seq):
    hd = w_out.shape[0]
    w = jnp.concatenate([w_qkv[:, :hd] * (Q_SCALE * LOG2_E), w_qkv[:, hd:]], axis=1).astype(BF16)
    (qkv,) = _norm_proj(x, g, w, [(3 * hd, BF16)])
    y = _stick_breaking(qkv, batch, seq, hd // HEAD_DIM)
    return _out_proj(_out_odd_kernel, x, [y], w_out.astype(BF16))


def kernel(x, norm_ffn1, w_ffn1_in, w_ffn1_out, norm_mix, w_in_ab, conv_w, cmp_pe_k, cmp_w1_k, cmp_w2_k,
           cmp_pe_v, cmp_w1_v, cmp_w2_v, w_out_ab, w_qkv_sb, w_out_sb, norm_ffn2, w_ffn2_in, w_ffn2_out,
           norm_final):
    batch, seq, d_model = x.shape
    depth = norm_ffn1.shape[0]
    x = x.reshape(batch * seq, d_model)
    for layer in range(depth):
        x = _ffn(x, norm_ffn1[layer], w_ffn1_in[layer].astype(BF16), w_ffn1_out[layer].astype(BF16))
        i = layer // 2
        if layer % 2 == 0:
            x = _conv_nsa_mixer(x, norm_mix[layer], w_in_ab[i], conv_w[i], cmp_pe_k[i], cmp_w1_k[i],
                                cmp_w2_k[i], cmp_pe_v[i], cmp_w1_v[i], cmp_w2_v[i], w_out_ab[i], batch, seq)
        else:
            x = _stick_breaking_mixer(x, norm_mix[layer], w_qkv_sb[i], w_out_sb[i], batch, seq)
        x = _ffn(x, norm_ffn2[layer], w_ffn2_in[layer].astype(BF16), w_ffn2_out[layer].astype(BF16))
    return _final_norm(x, norm_final).reshape(batch, seq, d_model)
```

```python
import functools

import jax
import jax.numpy as jnp
from jax import lax
from jax.experimental import pallas as pl
from jax.experimental.pallas import tpu as pltpu

F32 = jnp.float32
BF16 = jnp.bfloat16

EPS = 1e-6
NEG = -1e30
HEAD_DIM = 64
Q_SCALE = HEAD_DIM ** -0.5
LOG2_E = 1.4426950408889634
CONV_WIDTH = 3
NSA_KV_HEADS = 2
NSA_GROUP = 4
CMP_BLOCK = 32
CMP_STRIDE = 16
SEL_BLOCK = 64
SEL_TOPK = 16
WINDOW = 512
FORCE_BONUS = 1e4
LANES = 128
VMEM_LIMIT = 56 * 1024 * 1024

_NT = (((1,), (1,)), ((), ()))


def _params(*sem, flags=None):
    return pltpu.CompilerParams(dimension_semantics=sem, vmem_limit_bytes=VMEM_LIMIT, flags=flags)


def _rms(x, g):
    ms = jnp.mean(x * x, axis=-1, keepdims=True)
    return x * lax.rsqrt(ms + EPS) * g


def _dot(a, b):
    return jnp.dot(a, b, preferred_element_type=F32)


def _ffn_kernel(x_ref, g_ref, wi_ref, wo_ref, o_ref, *, tf):
    f = wo_ref.shape[0]
    x = x_ref[...]
    h = _rms(x, g_ref[...]).astype(BF16)
    acc = None
    for c in range(f // tf):
        gate = _dot(h, wi_ref[:, c * tf:(c + 1) * tf])
        up = _dot(h, wi_ref[:, f + c * tf:f + (c + 1) * tf])
        act = gate * jax.nn.sigmoid(gate) * up
        part = _dot(act.astype(BF16), wo_ref[c * tf:(c + 1) * tf, :])
        acc = part if acc is None else acc + part
    o_ref[...] = x + 0.5 * acc


def _ffn(x, g, w_in, w_out, tm=512):
    n, d = x.shape
    f = w_out.shape[0]
    tf = f // 2 if (f // 2) % LANES == 0 else f
    resident = pl.Buffered(1)
    return pl.pallas_call(
        functools.partial(_ffn_kernel, tf=tf),
        grid=(n // tm,),
        in_specs=[
            pl.BlockSpec((tm, d), lambda i: (i, 0)),
            pl.BlockSpec((1, d), lambda i: (0, 0)),
            pl.BlockSpec(w_in.shape, lambda i: (0, 0), pipeline_mode=resident),
            pl.BlockSpec(w_out.shape, lambda i: (0, 0), pipeline_mode=resident),
        ],
        out_specs=pl.BlockSpec((tm, d), lambda i: (i, 0)),
        out_shape=jax.ShapeDtypeStruct((n, d), F32),
        compiler_params=_params("parallel"),
        name="ffn",
    )(x, g.reshape(1, d), w_in, w_out)


def _proj_kernel(x_ref, g_ref, w_ref, *o_refs):
    h = _rms(x_ref[...], g_ref[...]).astype(BF16)
    off = 0
    for o_ref in o_refs:
        width = o_ref.shape[1]
        o_ref[...] = _dot(h, w_ref[:, off:off + width]).astype(o_ref.dtype)
        off += width


def _norm_proj(x, g, w, outs, tm=512):
    n, d = x.shape
    return pl.pallas_call(
        _proj_kernel,
        grid=(n // tm,),
        in_specs=[
            pl.BlockSpec((tm, d), lambda i: (i, 0)),
            pl.BlockSpec((1, d), lambda i: (0, 0)),
            pl.BlockSpec(w.shape, lambda i: (0, 0)),
        ],
        out_specs=[pl.BlockSpec((tm, wd), lambda i: (i, 0)) for wd, _ in outs],
        out_shape=[jax.ShapeDtypeStruct((n, wd), dt) for wd, dt in outs],
        compiler_params=_params("parallel"),
        name="norm_proj",
    )(x, g.reshape(1, d), w)


def _conv_kernel(cv_ref, prev_ref, w_ref, o_ref, ext_ref, *, tiles_per_seq, cd):
    tm = o_ref.shape[0]
    first = (pl.program_id(0) % tiles_per_seq) == 0
    u = cv_ref[:, cd:2 * cd] * cv_ref[:, 2 * cd:3 * cd]
    u_prev = prev_ref[:, cd:2 * cd] * prev_ref[:, 2 * cd:3 * cd]
    ext_ref[0:8, :] = jnp.where(first, 0.0, u_prev)
    ext_ref[8:, :] = u
    w = w_ref[...]
    y = w[0:1] * ext_ref[6:6 + tm, :] + w[1:2] * ext_ref[7:7 + tm, :] + w[2:3] * u
    o_ref[...] = (cv_ref[:, 0:cd] * y).astype(o_ref.dtype)


def _short_conv(cv, conv_w, seq, tm=512):
    n = cv.shape[0]
    cd = conv_w.shape[1]
    kern = functools.partial(_conv_kernel, tiles_per_seq=seq // tm, cd=cd)
    return pl.pallas_call(
        kern,
        grid=(n // tm,),
        in_specs=[
            pl.BlockSpec((tm, 3 * cd), lambda i: (i, 0)),
            pl.BlockSpec((8, 3 * cd), lambda i: (jnp.maximum(i * (tm // 8) - 1, 0), 0)),
            pl.BlockSpec(conv_w.shape, lambda i: (0, 0)),
        ],
        out_specs=pl.BlockSpec((tm, cd), lambda i: (i, 0)),
        out_shape=jax.ShapeDtypeStruct((n, cd), BF16),
        scratch_shapes=[pltpu.VMEM((tm + 8, cd), F32)],
        compiler_params=_params("parallel"),
        name="short_conv",
    )(cv, cv, conv_w)


def _compress_kernel(cm_ref, w1_ref, w2_ref, pe_ref, o_ref):
    nch = o_ref.shape[0]
    half = cm_ref.shape[1]
    outs = []
    for t in range(2):
        cm = cm_ref[t * nch:(t + 1) * nch, :]
        a = _dot(cm, w1_ref[t, 0:half, :])
        b = _dot(cm, w1_ref[t, half:2 * half, :])
        pe = jnp.broadcast_to(pe_ref[t], (8, 2 * half)).astype(BF16)
        bias = _dot(pe, w1_ref[t])[0:1]
        hid = a + pltpu.roll(b, nch - 1, 0) + bias
        outs.append(_dot(jax.nn.gelu(hid).astype(BF16), w2_ref[t]))
    o_ref[...] = jnp.concatenate(outs, axis=1).astype(o_ref.dtype)


def _compress(cm, w1, w2, pe, nch):
    rows, half = cm.shape
    bg = rows // (2 * nch)
    d = w2.shape[2]
    return pl.pallas_call(
        _compress_kernel,
        grid=(bg,),
        in_specs=[
            pl.BlockSpec((2 * nch, half), lambda i: (i, 0)),
            pl.BlockSpec(w1.shape, lambda i: (0, 0, 0)),
            pl.BlockSpec(w2.shape, lambda i: (0, 0, 0)),
            pl.BlockSpec(pe.shape, lambda i: (0, 0, 0)),
        ],
        out_specs=pl.BlockSpec((nch, 2 * d), lambda i: (i, 0)),
        out_shape=jax.ShapeDtypeStruct((bg * nch, 2 * d), BF16),
        compiler_params=_params("parallel"),
        name="compress",
    )(cm, w1, w2, pe)


def _cmp_kernel(q_ref, kvc_ref, ovl_ref, gt_ref, o_ref, sel_ref, *, nblk):
    tq = q_ref.shape[0]
    nc = kvc_ref.shape[0]
    d = HEAD_DIM
    i = pl.program_id(2)
    q = q_ref[...]
    kvc = kvc_ref[...]
    kc = kvc[:, 0:d]
    pos = i * tq + lax.broadcasted_iota(jnp.int32, (tq, nc), 0)
    cidx = lax.broadcasted_iota(jnp.int32, (tq, nc), 1)
    cmask = cidx * CMP_STRIDE + (CMP_BLOCK - 1) <= pos
    gate = jax.nn.sigmoid(gt_ref[...])
    imp = jnp.zeros((tq, LANES), F32)
    outs = []
    for r in range(NSA_GROUP):
        s = lax.dot_general(q[:, r * d:(r + 1) * d], kc, _NT, preferred_element_type=F32)
        s = jnp.where(cmask, s, NEG)
        e = jnp.exp2(s - jnp.max(s, axis=1, keepdims=True))
        p = jnp.where(cmask, e / jnp.sum(e, axis=1, keepdims=True), 0.0)
        pb = p.astype(BF16)
        o = _dot(pb, kvc)[:, d:2 * d]
        outs.append(o * gate[:, 3 * r:3 * r + 1])
        imp = imp + _dot(pb, ovl_ref[...])
    o_ref[...] = jnp.concatenate(outs, axis=1)

    posb = i * tq + lax.broadcasted_iota(jnp.int32, (tq, LANES), 0)
    blk = lax.broadcasted_iota(jnp.int32, (tq, LANES), 1)
    cur = posb >> 6
    forced = jnp.where(blk == 0, 1.0, jnp.where(blk == cur, 1.0, jnp.where(blk == cur - 1, 1.0, 0.0)))
    score = jnp.where(blk * SEL_BLOCK <= posb, imp + forced * FORCE_BONUS, NEG)
    sc = score.T[0:nblk, :]
    grp = 8
    sub = lax.broadcasted_iota(jnp.int32, (grp, tq), 0)
    groups = [sc[g0:g0 + grp, :] for g0 in range(0, nblk, grp)]
    ranks = [jnp.zeros((grp, tq), F32) for _ in groups]
    for sp in range(nblk):
        other = sc[sp:sp + 1, :]
        for gi, mine in enumerate(groups):
            g0 = gi * grp
            ge = jnp.where(other >= mine, 1.0, 0.0)
            gt = jnp.where(other > mine, 1.0, 0.0)
            if g0 > sp:
                beats = ge
            elif g0 + grp - 1 < sp:
                beats = gt
            else:
                beats = jnp.where(sub + g0 > sp, ge, gt)
            ranks[gi] = ranks[gi] + beats
    chosen = jnp.where(jnp.concatenate(ranks, axis=0) < SEL_TOPK, 1.0, 0.0)
    chosen = jnp.concatenate([chosen, jnp.zeros((LANES - nblk, tq), F32)], axis=0)
    sel_ref[...] = chosen.T.astype(sel_ref.dtype)


def _cmp_branch(q, kvc, ovl, gates, batch, seq, tq=512):
    n = q.shape[0]
    nt = seq // tq
    g = NSA_KV_HEADS
    wq = NSA_GROUP * HEAD_DIM
    nc = kvc.shape[0] // (batch * g)
    kern = functools.partial(_cmp_kernel, nblk=seq // SEL_BLOCK)
    return pl.pallas_call(
        kern,
        grid=(batch, g, nt),
        in_specs=[
            pl.BlockSpec((tq, wq), lambda b, h, i: (b * nt + i, h)),
            pl.BlockSpec((nc, 2 * HEAD_DIM), lambda b, h, i: (b * g + h, 0)),
            pl.BlockSpec(ovl.shape, lambda b, h, i: (0, 0)),
            pl.BlockSpec((tq, LANES), lambda b, h, i: (b * nt + i, h)),
        ],
        out_specs=[
            pl.BlockSpec((tq, wq), lambda b, h, i: (b * nt + i, h)),
            pl.BlockSpec((tq, LANES), lambda b, h, i: ((b * g + h) * nt + i, 0)),
        ],
        out_shape=[
            jax.ShapeDtypeStruct((n, g * wq), F32),
            jax.ShapeDtypeStruct((batch * g * seq, LANES), BF16),
        ],
        compiler_params=_params("parallel", "parallel", "parallel"),
        name="nsa_cmp_select",
    )(q, kvc, ovl, gates)


NSA_STAGES = 3
NSA_CHUNK = 32
MASK_BIG = 1e30


def _nsa_attn_kernel(ti_ref, tj_ref, tm_ref, tf_ref, q_ref, kv_ref, *rest, mode, branch, tq, n_units):
    if mode == "sel":
        (sel_ref, gt_ref, o_ref, qh_ref, kk_ref, va_ref, ex_ref, bias_ref, s_ref, p_ref, al_ref, m_ref, acc_ref,
         fin_ref) = rest
    else:
        gt_ref, o_ref, qh_ref, kk_ref, va_ref, bias_ref, s_ref, p_ref, al_ref, m_ref, acc_ref, fin_ref = rest
    tk = tq
    d = HEAD_DIM
    seq = q_ref.shape[0]
    row = lax.broadcasted_iota(jnp.int32, (tq, tk), 0)
    col = lax.broadcasted_iota(jnp.int32, (tq, tk), 1)
    bias_ref[0] = jnp.zeros((tq, tk), F32)
    bias_ref[1] = jnp.where(col <= row, 0.0, NEG)
    bias_ref[2] = jnp.where(col > row, 0.0, NEG)
    s_ref[...] = jnp.zeros_like(s_ref)
    p_ref[...] = jnp.zeros_like(p_ref)
    al_ref[...] = jnp.zeros_like(al_ref)
    acc_ref[...] = jnp.zeros_like(acc_ref)
    m_ref[...] = jnp.full(m_ref.shape, NEG, F32)
    lane = lax.broadcasted_iota(jnp.int32, (tq, 2 * d), 1)
    srow = lax.broadcasted_iota(jnp.int32, (LANES, tk), 0)
    scol = lax.broadcasted_iota(jnp.int32, (LANES, tk), 1) >> 6

    def prep(c, carry):
        rows = pl.ds(pl.multiple_of(c * tq, tq), tq)
        q = q_ref[rows, :]
        for r in range(NSA_GROUP):
            pair = q[:, (r // 2) * 2 * d:(r // 2 + 1) * 2 * d]
            mine = (lane < d) if r % 2 == 0 else (lane >= d)
            qh_ref[r, rows, :] = jnp.where(mine, pair, 0.0).astype(BF16)
        kv = kv_ref[rows, :]
        k_first = kv[:, 0:d]
        kk_ref[rows, :] = jnp.concatenate([k_first, k_first], axis=1)
        v_first = jnp.concatenate([kv[:, d:2 * d], k_first], axis=1)
        va_ref[rows, :] = jnp.where(lane < d, v_first, 1.0).astype(BF16)
        if mode == "sel":
            ex_ref[c] = jnp.where(srow == c * (tk // SEL_BLOCK) + scol, 1.0, 0.0).astype(BF16)
        return carry

    lax.fori_loop(0, seq // tq, prep, 0)

    def unit(s, delay):
        u = jnp.clip(s - delay, 0, n_units - 1)
        return ti_ref[u], tj_ref[u], tm_ref[u], tf_ref[u]

    def step(s, carry):
        i_c, j_c, _, f_c = unit(s, 2)
        va = va_ref[pl.ds(pl.multiple_of(j_c * tk, tk), tk), :]
        keep = jnp.where((f_c & 1) == 1, 0.0, 1.0)
        for r in range(NSA_GROUP):
            acc = acc_ref[r] * (al_ref[r] * keep) + _dot(p_ref[r], va)
            acc_ref[r] = acc
            fin_ref[i_c, r] = acc
        _, _, _, f_b = unit(s, 1)
        first_b = (f_b & 1) == 1
        for r in range(NSA_GROUP):
            for c in range(tq // NSA_CHUNK):
                rows = slice(c * NSA_CHUNK, (c + 1) * NSA_CHUNK)
                sc = s_ref[r, rows, :]
                m_prev = jnp.where(first_b, NEG, m_ref[r, rows, :])
                m_new = jnp.maximum(m_prev, jnp.max(sc, axis=1, keepdims=True))
                al_ref[r, rows, :] = jnp.exp2(m_prev - m_new)
                m_ref[r, rows, :] = m_new
                p_ref[r, rows, :] = jnp.exp2(sc - jnp.concatenate([m_new, m_new], axis=1)).astype(BF16)
        i_a, j_a, t_a, _ = unit(s, 0)
        q_rows = pl.ds(pl.multiple_of(i_a * tq, tq), tq)
        if mode == "sel":
            chosen = _dot(sel_ref[q_rows, :], ex_ref[j_a])
            bias_ref[3] = bias_ref[t_a] + (chosen - 1.0) * MASK_BIG
            t_a = 3
        k = kk_ref[pl.ds(pl.multiple_of(j_a * tk, tk), tk), :]
        for r in range(NSA_GROUP):
            s_ref[r] = (lax.dot_general(qh_ref[r, q_rows, :], k, _NT, preferred_element_type=F32)
                        + bias_ref[t_a])

        return carry

    lax.fori_loop(0, n_units + NSA_STAGES - 1, step, 0, unroll=2 if mode == "win" else 1)

    def finish(c, carry):
        o_rows = pl.ds(pl.multiple_of(c * tq, tq), tq)
        gate = jax.nn.sigmoid(gt_ref[o_rows, :])
        outs = []
        for r in range(NSA_GROUP):
            acc = fin_ref[c, r]
            outs.append(acc[:, 0:d] / acc[:, d:d + 1] * gate[:, 3 * r + branch:3 * r + branch + 1])
        o_ref[o_rows, :] = jnp.concatenate(outs, axis=1)
        return carry

    lax.fori_loop(0, seq // tq, finish, 0)


def _nsa_attn(q, kv, sel, gates, batch, seq, mode, tq=256):
    n = q.shape[0]
    nt = seq // tq
    g = NSA_KV_HEADS
    wq = NSA_GROUP * HEAD_DIM
    branch = 1 if mode == "sel" else 2
    assert WINDOW == 2 * tq
    if mode == "sel":
        units = [(i, j, int(j == i), int(j == 0) | 2 * int(j == i)) for i in range(nt) for j in range(i + 1)]
    else:
        units = [(i, j, (1, 0, 2)[i - j], int(j == max(i - 2, 0)) | 2 * int(j == i))
                 for i in range(nt) for j in range(max(i - 2, 0), i + 1)]
    tables = [jnp.array([u[c] for u in units], jnp.int32) for c in range(4)]
    kern = functools.partial(_nsa_attn_kernel, mode=mode, branch=branch, tq=tq, n_units=len(units))
    in_specs = [
        pl.BlockSpec((seq, wq), lambda b, h, *_: (b, h)),
        pl.BlockSpec((seq, 2 * HEAD_DIM), lambda b, h, *_: (b, h * 3 + branch)),
    ]
    args = [q, kv]
    scratch = [
        pltpu.VMEM((NSA_GROUP, seq, 2 * HEAD_DIM), BF16),
        pltpu.VMEM((seq, 2 * HEAD_DIM), BF16),
        pltpu.VMEM((seq, 2 * HEAD_DIM), BF16),
    ]
    if mode == "sel":
        in_specs.append(pl.BlockSpec((seq, LANES), lambda b, h, *_: (b * g + h, 0)))
        args.append(sel)
        scratch.append(pltpu.VMEM((nt, LANES, tq), BF16))
    in_specs.append(pl.BlockSpec((seq, LANES), lambda b, h, *_: (b, h)))
    args.append(gates)
    scratch += [
        pltpu.VMEM((4, tq, tq), F32),
        pltpu.VMEM((NSA_GROUP, tq, tq), F32),
        pltpu.VMEM((NSA_GROUP, tq, tq), BF16),
        pltpu.VMEM((NSA_GROUP, tq, 2 * HEAD_DIM), F32),
        pltpu.VMEM((NSA_GROUP, tq, 2 * HEAD_DIM), F32),
        pltpu.VMEM((NSA_GROUP, tq, 2 * HEAD_DIM), F32),
        pltpu.VMEM((nt, NSA_GROUP, tq, 2 * HEAD_DIM), F32),
    ]
    grid_spec = pltpu.PrefetchScalarGridSpec(
        num_scalar_prefetch=4,
        grid=(batch, g),
        in_specs=in_specs,
        out_specs=pl.BlockSpec((seq, wq), lambda b, h, *_: (b, h)),
        scratch_shapes=scratch,
    )
    return pl.pallas_call(
        kern,
        grid_spec=grid_spec,
        out_shape=jax.ShapeDtypeStruct((n, g * wq), F32),
        compiler_params=_params("parallel", "parallel"),
        name="nsa_" + mode,
    )(*tables, *args)


SB_STAGES = 4
SB_CHUNK = 32


def _sb_kernel(ti_ref, tj_ref, q_ref, k_ref, v_ref, o_ref, qh_ref, vt_ref, later_ref, bias_ref,
               z_ref, hl_ref, zl_ref, a_ref, acc_ref, carry_ref, fin_ref, *, tq, n_units):
    tk = tq
    d = HEAD_DIM
    seq = q_ref.shape[0]
    key = lax.broadcasted_iota(jnp.int32, (tk, tq), 0)
    qry = lax.broadcasted_iota(jnp.int32, (tk, tq), 1)
    later_ref[...] = jnp.where(qry > key, 1.0, 0.0).astype(BF16)
    bias_ref[0] = jnp.zeros((tk, tq), F32)
    bias_ref[1] = jnp.where(key < qry, 0.0, NEG)
    z_ref[...] = jnp.zeros_like(z_ref)
    hl_ref[...] = jnp.zeros_like(hl_ref)
    zl_ref[...] = jnp.zeros_like(zl_ref)
    a_ref[...] = jnp.zeros_like(a_ref)
    acc_ref[...] = jnp.zeros_like(acc_ref)
    carry_ref[...] = jnp.zeros_like(carry_ref)

    lane = lax.broadcasted_iota(jnp.int32, (tq, 2 * d), 1)

    def split(c, carry):
        rows = pl.ds(pl.multiple_of(c * tq, tq), tq)
        q = q_ref[rows, :]
        qh_ref[0, rows, :] = jnp.where(lane < d, q, 0.0).astype(BF16)
        qh_ref[1, rows, :] = jnp.where(lane >= d, q, 0.0).astype(BF16)
        vt_ref[c] = v_ref[rows, :].astype(F32).T.astype(BF16)
        return carry

    lax.fori_loop(0, seq // tq, split, 0)
    feat = lax.broadcasted_iota(jnp.int32, (2 * d, tq), 0)

    def unit(s, delay):
        u = jnp.clip(s - delay, 0, n_units - 1)
        return ti_ref[u], tj_ref[u]

    def step(s, carry):
        i_e, j_e = unit(s, 3)
        vt = vt_ref[j_e]
        keep = jnp.where(i_e == j_e, 0.0, 1.0)
        accs = []
        for h in range(2):
            acc = acc_ref[h] * keep + _dot(vt, a_ref[h])
            acc_ref[h] = acc
            accs.append(acc)
        fin_ref[i_e] = jnp.where(feat < d, accs[0], accs[1])
        for h in range(2):
            between = _dot(later_ref[...], hl_ref[h])
            for c in range(tk // SB_CHUNK):
                rows = slice(c * SB_CHUNK, (c + 1) * SB_CHUNK)
                a_ref[h, rows, :] = jnp.exp2(zl_ref[h, rows, :] + between[rows]).astype(BF16)
        i_b, j_b = unit(s, 1)
        keep_b = jnp.where(i_b == j_b, 0.0, 1.0)
        for h in range(2):
            before = carry_ref[h] * keep_b
            part = jnp.zeros((8, tq), F32)
            for c in range(tk // SB_CHUNK):
                rows = slice(c * SB_CHUNK, (c + 1) * SB_CHUNK)
                z = z_ref[h, rows, :]
                nz = -z
                log_stay = jnp.minimum(nz, 0.0) - jnp.log(1.0 + jnp.exp2(jnp.minimum(z, nz))) * LOG2_E
                hl_ref[h, rows, :] = log_stay.astype(BF16)
                zl_ref[h, rows, :] = z + log_stay + before
                part = part + jnp.sum(log_stay.reshape(SB_CHUNK // 8, 8, tq), axis=0)
            carry_ref[h] = before + jnp.sum(part, axis=0, keepdims=True)
        i_a, j_a = unit(s, 0)
        bias = bias_ref[jnp.where(i_a == j_a, 1, 0)]
        q_rows = pl.ds(pl.multiple_of(i_a * tq, tq), tq)
        k_rows = pl.ds(pl.multiple_of(j_a * tk, tk), tk)
        for h in range(2):
            z_ref[h] = lax.dot_general(k_ref[k_rows, :], qh_ref[h, q_rows, :], _NT,
                                       preferred_element_type=F32) + bias
        return carry

    lax.fori_loop(0, n_units + SB_STAGES - 1, step, 0, unroll=2)

    def finish(c, carry):
        o_ref[pl.ds(pl.multiple_of(c * tq, tq), tq), :] = fin_ref[c].T.astype(o_ref.dtype)
        return carry

    lax.fori_loop(0, seq // tq, finish, 0)


def _stick_breaking(qkv, batch, seq, heads, tq=256):
    n = qkv.shape[0]
    nt = seq // tq
    pairs = heads // 2
    units = [(i, j) for i in range(nt) for j in range(i, -1, -1)]
    ti = jnp.array([u[0] for u in units], jnp.int32)
    tj = jnp.array([u[1] for u in units], jnp.int32)
    kern = functools.partial(_sb_kernel, tq=tq, n_units=len(units))
    grid_spec = pltpu.PrefetchScalarGridSpec(
        num_scalar_prefetch=2,
        grid=(batch, pairs),
        in_specs=[
            pl.BlockSpec((seq, LANES), lambda b, h, ti, tj: (b, h)),
            pl.BlockSpec((seq, LANES), lambda b, h, ti, tj: (b, pairs + h)),
            pl.BlockSpec((seq, LANES), lambda b, h, ti, tj: (b, 2 * pairs + h)),
        ],
        out_specs=pl.BlockSpec((seq, LANES), lambda b, h, ti, tj: (b, h)),
        scratch_shapes=[
            pltpu.VMEM((2, seq, 2 * HEAD_DIM), BF16),
            pltpu.VMEM((nt, 2 * HEAD_DIM, tq), BF16),
            pltpu.VMEM((tq, tq), BF16),
            pltpu.VMEM((2, tq, tq), F32),
            pltpu.VMEM((2, tq, tq), F32),
            pltpu.VMEM((2, tq, tq), BF16),
            pltpu.VMEM((2, tq, tq), F32),
            pltpu.VMEM((2, tq, tq), BF16),
            pltpu.VMEM((2, 2 * HEAD_DIM, tq), F32),
            pltpu.VMEM((2, 1, tq), F32),
            pltpu.VMEM((nt, 2 * HEAD_DIM, tq), F32),
        ],
    )
    return pl.pallas_call(
        kern,
        grid_spec=grid_spec,
        out_shape=jax.ShapeDtypeStruct((n, heads * HEAD_DIM), BF16),
        compiler_params=_params("parallel", "parallel"),
        name="stick_breaking",
    )(ti, tj, qkv, qkv, qkv)


def _out_even_kernel(x_ref, yc_ref, oc_ref, os_ref, ow_ref, w_ref, o_ref):
    cd = yc_ref.shape[1]
    y_nsa = (oc_ref[...] + os_ref[...] + ow_ref[...]).astype(BF16)
    o_ref[...] = x_ref[...] + _dot(yc_ref[...], w_ref[0:cd, :]) + _dot(y_nsa, w_ref[cd:, :])


def _out_odd_kernel(x_ref, y_ref, w_ref, o_ref):
    o_ref[...] = x_ref[...] + _dot(y_ref[...], w_ref[...])


def _out_proj(kern, x, ys, w, tm=512):
    n, d = x.shape
    return pl.pallas_call(
        kern,
        grid=(n // tm,),
        in_specs=[pl.BlockSpec((tm, d), lambda i: (i, 0))]
        + [pl.BlockSpec((tm, y.shape[1]), lambda i: (i, 0)) for y in ys]
        + [pl.BlockSpec(w.shape, lambda i: (0, 0))],
        out_specs=pl.BlockSpec((tm, d), lambda i: (i, 0)),
        out_shape=jax.ShapeDtypeStruct((n, d), F32),
        compiler_params=_params("parallel"),
        name="out_proj",
    )(x, *ys, w)


def _final_norm_kernel(x_ref, g_ref, o_ref):
    o_ref[...] = _rms(x_ref[...], g_ref[...])


def _final_norm(x, g, tm=512):
    n, d = x.shape
    return pl.pallas_call(
        _final_norm_kernel,
        grid=(n // tm,),
        in_specs=[pl.BlockSpec((tm, d), lambda i: (i, 0)), pl.BlockSpec((1, d), lambda i: (0, 0))],
        out_specs=pl.BlockSpec((tm, d), lambda i: (i, 0)),
        out_shape=jax.ShapeDtypeStruct((n, d), F32),
        compiler_params=_params("parallel"),
        name="final_norm",
    )(x, g.reshape(1, d))


def _overlap_matrix(nch, nblk):
    c0 = jnp.arange(nch) * CMP_STRIDE
    s0 = jnp.arange(LANES) * SEL_BLOCK
    lo = jnp.maximum(c0[:, None], s0[None, :])
    hi = jnp.minimum(c0[:, None] + CMP_BLOCK, s0[None, :] + SEL_BLOCK)
    ovl = jnp.maximum(hi - lo, 0).astype(F32) / CMP_BLOCK
    keep = (jnp.arange(nch)[:, None] < nch - 1) & (jnp.arange(LANES)[None, :] < nblk)
    return jnp.where(keep, ovl, 0.0).astype(BF16)


def _conv_nsa_mixer(x, g, w_in, conv_w, pe_k, w1_k, w2_k, pe_v, w1_v, w2_v, w_out, batch, seq):
    d = HEAD_DIM
    cd = conv_w.shape[1]
    kvd = NSA_KV_HEADS * d
    o_q = 3 * cd
    o_kv = o_q + NSA_KV_HEADS * NSA_GROUP * d
    o_g = o_kv + 6 * kvd
    ng = NSA_GROUP * 3
    kv_cols = []
    gate_cols = []
    for h in range(NSA_KV_HEADS):
        for typ in range(3):
            base = o_kv + typ * 2 * kvd + h * d
            kv_cols += [w_in[:, base:base + d], w_in[:, base + kvd:base + kvd + d]]
        gate_cols += [w_in[:, o_g + h * ng:o_g + (h + 1) * ng], jnp.zeros((w_in.shape[0], LANES - ng), F32)]
    w_all = jnp.concatenate(
        [w_in[:, :o_q], w_in[:, o_q:o_kv] * (Q_SCALE * LOG2_E)] + kv_cols + gate_cols, axis=1).astype(BF16)
    cv, q, kv, gates = _norm_proj(
        x, g, w_all,
        [(3 * cd, F32), (o_kv - o_q, BF16), (6 * kvd, BF16), (NSA_KV_HEADS * LANES, F32)])

    y_conv = _short_conv(cv, conv_w, seq)

    nch = seq // CMP_STRIDE
    kv6 = kv.reshape(batch, seq, NSA_KV_HEADS, 3, 2, d)
    cm = kv6[:, :, :, 0].transpose(0, 2, 3, 1, 4).reshape(batch * NSA_KV_HEADS * 2 * nch, CMP_STRIDE * d)
    w1 = jnp.stack([w1_k, w1_v]).astype(BF16)
    w2 = jnp.stack([w2_k, w2_v]).astype(BF16)
    pe = jnp.stack([pe_k.reshape(1, -1), pe_v.reshape(1, -1)])
    kvc = _compress(cm, w1, w2, pe, nch)

    ovl = _overlap_matrix(nch, seq // SEL_BLOCK)
    o_cmp, sel = _cmp_branch(q, kvc, ovl, gates, batch, seq)
    o_sel = _nsa_attn(q, kv, sel, gates, batch, seq, "sel")
    o_win = _nsa_attn(q, kv, sel, gates, batch, seq, "win")
    return _out_proj(_out_even_kernel, x, [y_conv, o_cmp, o_sel, o_win], w_out.astype(BF16))


def _stick_breaking_mixer(x, g, w_qkv, w_out, batch, seq):
    hd = w_out.shape[0]
    w = jnp.concatenate([w_qkv[:, :hd] * (Q_SCALE * LOG2_E), w_qkv[:, hd:]], axis=1).astype(BF16)
    (qkv,) = _norm_proj(x, g, w, [(3 * hd, BF16)])
    y = _stick_breaking(qkv, batch, seq, hd // HEAD_DIM)
    return _out_proj(_out_odd_kernel, x, [y], w_out.astype(BF16))


def kernel(x, norm_ffn1, w_ffn1_in, w_ffn1_out, norm_mix, w_in_ab, conv_w, cmp_pe_k, cmp_w1_k, cmp_w2_k,
           cmp_pe_v, cmp_w1_v, cmp_w2_v, w_out_ab, w_qkv_sb, w_out_sb, norm_ffn2, w_ffn2_in, w_ffn2_out,
           norm_final):
    batch, seq, d_model = x.shape
    depth = norm_ffn1.shape[0]
    x = x.reshape(batch * seq, d_model)
    for layer in range(depth):
        x = _ffn(x, norm_ffn1[layer], w_ffn1_in[layer].astype(BF16), w_ffn1_out[layer].astype(BF16))
        i = layer // 2
        if layer % 2 == 0:
            x = _conv_nsa_mixer(x, norm_mix[layer], w_in_ab[i], conv_w[i], cmp_pe_k[i], cmp_w1_k[i],
                                cmp_w2_k[i], cmp_pe_v[i], cmp_w1_v[i], cmp_w2_v[i], w_out_ab[i], batch, seq)
        else:
            x = _stick_breaking_mixer(x, norm_mix[layer], w_qkv_sb[i], w_out_sb[i], batch, seq)
        x = _ffn(x, norm_ffn2[layer], w_ffn2_in[layer].astype(BF16), w_ffn2_out[layer].astype(BF16))
    return _final_norm(x, norm_final).reshape(batch, seq, d_model)
```

```python
import functools

import jax
import jax.numpy as jnp
from jax import lax
from jax.experimental import pallas as pl
from jax.experimental.pallas import tpu as pltpu

F32 = jnp.float32
BF16 = jnp.bfloat16

EPS = 1e-6
NEG = -1e30
HEAD_DIM = 64
Q_SCALE = HEAD_DIM ** -0.5
LOG2_E = 1.4426950408889634
CONV_WIDTH = 3
NSA_KV_HEADS = 2
NSA_GROUP = 4
CMP_BLOCK = 32
CMP_STRIDE = 16
SEL_BLOCK = 64
SEL_TOPK = 16
WINDOW = 512
FORCE_BONUS = 1e4
LANES = 128
VMEM_LIMIT = 56 * 1024 * 1024

_NT = (((1,), (1,)), ((), ()))


def _params(*sem, flags=None):
    return pltpu.CompilerParams(dimension_semantics=sem, vmem_limit_bytes=VMEM_LIMIT, flags=flags)


def _rms(x, g):
    ms = jnp.mean(x * x, axis=-1, keepdims=True)
    return x * lax.rsqrt(ms + EPS) * g


def _dot(a, b):
    return jnp.dot(a, b, preferred_element_type=F32)


def _ffn_kernel(x_ref, g_ref, wi_ref, wo_ref, o_ref, *, tf):
    f = wo_ref.shape[0]
    x = x_ref[...]
    h = _rms(x, g_ref[...]).astype(BF16)
    acc = None
    for c in range(f // tf):
        gate = _dot(h, wi_ref[:, c * tf:(c + 1) * tf])
        up = _dot(h, wi_ref[:, f + c * tf:f + (c + 1) * tf])
        act = gate * jax.nn.sigmoid(gate) * up
        part = _dot(act.astype(BF16), wo_ref[c * tf:(c + 1) * tf, :])
        acc = part if acc is None else acc + part
    o_ref[...] = x + 0.5 * acc


def _ffn(x, g, w_in, w_out, tm=512):
    n, d = x.shape
    f = w_out.shape[0]
    tf = f // 2 if (f // 2) % LANES == 0 else f
    resident = pl.Buffered(1)
    return pl.pallas_call(
        functools.partial(_ffn_kernel, tf=tf),
        grid=(n // tm,),
        in_specs=[
            pl.BlockSpec((tm, d), lambda i: (i, 0)),
            pl.BlockSpec((1, d), lambda i: (0, 0)),
            pl.BlockSpec(w_in.shape, lambda i: (0, 0), pipeline_mode=resident),
            pl.BlockSpec(w_out.shape, lambda i: (0, 0), pipeline_mode=resident),
        ],
        out_specs=pl.BlockSpec((tm, d), lambda i: (i, 0)),
        out_shape=jax.ShapeDtypeStruct((n, d), F32),
        compiler_params=_params("parallel"),
        name="ffn",
    )(x, g.reshape(1, d), w_in, w_out)


def _proj_kernel(x_ref, g_ref, w_ref, *o_refs):
    h = _rms(x_ref[...], g_ref[...]).astype(BF16)
    off = 0
    for o_ref in o_refs:
        width = o_ref.shape[1]
        o_ref[...] = _dot(h, w_ref[:, off:off + width]).astype(o_ref.dtype)
        off += width


def _norm_proj(x, g, w, outs, tm=512):
    n, d = x.shape
    return pl.pallas_call(
        _proj_kernel,
        grid=(n // tm,),
        in_specs=[
            pl.BlockSpec((tm, d), lambda i: (i, 0)),
            pl.BlockSpec((1, d), lambda i: (0, 0)),
            pl.BlockSpec(w.shape, lambda i: (0, 0)),
        ],
        out_specs=[pl.BlockSpec((tm, wd), lambda i: (i, 0)) for wd, _ in outs],
        out_shape=[jax.ShapeDtypeStruct((n, wd), dt) for wd, dt in outs],
        compiler_params=_params("parallel"),
        name="norm_proj",
    )(x, g.reshape(1, d), w)


def _conv_kernel(cv_ref, prev_ref, w_ref, o_ref, ext_ref, *, tiles_per_seq, cd):
    tm = o_ref.shape[0]
    first = (pl.program_id(0) % tiles_per_seq) == 0
    u = cv_ref[:, cd:2 * cd] * cv_ref[:, 2 * cd:3 * cd]
    u_prev = prev_ref[:, cd:2 * cd] * prev_ref[:, 2 * cd:3 * cd]
    ext_ref[0:8, :] = jnp.where(first, 0.0, u_prev)
    ext_ref[8:, :] = u
    w = w_ref[...]
    y = w[0:1] * ext_ref[6:6 + tm, :] + w[1:2] * ext_ref[7:7 + tm, :] + w[2:3] * u
    o_ref[...] = (cv_ref[:, 0:cd] * y).astype(o_ref.dtype)


def _short_conv(cv, conv_w, seq, tm=512):
    n = cv.shape[0]
    cd = conv_w.shape[1]
    kern = functools.partial(_conv_kernel, tiles_per_seq=seq // tm, cd=cd)
    return pl.pallas_call(
        kern,
        grid=(n // tm,),
        in_specs=[
            pl.BlockSpec((tm, 3 * cd), lambda i: (i, 0)),
            pl.BlockSpec((8, 3 * cd), lambda i: (jnp.maximum(i * (tm // 8) - 1, 0), 0)),
            pl.BlockSpec(conv_w.shape, lambda i: (0, 0)),
        ],
        out_specs=pl.BlockSpec((tm, cd), lambda i: (i, 0)),
        out_shape=jax.ShapeDtypeStruct((n, cd), BF16),
        scratch_shapes=[pltpu.VMEM((tm + 8, cd), F32)],
        compiler_params=_params("parallel"),
        name="short_conv",
    )(cv, cv, conv_w)


def _compress_kernel(cm_ref, w1_ref, w2_ref, pe_ref, o_ref):
    nch = o_ref.shape[0]
    half = cm_ref.shape[1]
    outs = []
    for t in range(2):
        cm = cm_ref[t * nch:(t + 1) * nch, :]
        a = _dot(cm, w1_ref[t, 0:half, :])
        b = _dot(cm, w1_ref[t, half:2 * half, :])
        pe = jnp.broadcast_to(pe_ref[t], (8, 2 * half)).astype(BF16)
        bias = _dot(pe, w1_ref[t])[0:1]
        hid = a + pltpu.roll(b, nch - 1, 0) + bias
        outs.append(_dot(jax.nn.gelu(hid).astype(BF16), w2_ref[t]))
    o_ref[...] = jnp.concatenate(outs, axis=1).astype(o_ref.dtype)


def _compress(cm, w1, w2, pe, nch):
    rows, half = cm.shape
    bg = rows // (2 * nch)
    d = w2.shape[2]
    return pl.pallas_call(
        _compress_kernel,
        grid=(bg,),
        in_specs=[
            pl.BlockSpec((2 * nch, half), lambda i: (i, 0)),
            pl.BlockSpec(w1.shape, lambda i: (0, 0, 0)),
            pl.BlockSpec(w2.shape, lambda i: (0, 0, 0)),
            pl.BlockSpec(pe.shape, lambda i: (0, 0, 0)),
        ],
        out_specs=pl.BlockSpec((nch, 2 * d), lambda i: (i, 0)),
        out_shape=jax.ShapeDtypeStruct((bg * nch, 2 * d), BF16),
        compiler_params=_params("parallel"),
        name="compress",
    )(cm, w1, w2, pe)


def _cmp_kernel(q_ref, kvc_ref, ovl_ref, gt_ref, o_ref, sel_ref, *, nblk):
    tq = q_ref.shape[0]
    nc = kvc_ref.shape[0]
    d = HEAD_DIM
    i = pl.program_id(2)
    q = q_ref[...]
    kvc = kvc_ref[...]
    kc = kvc[:, 0:d]
    pos = i * tq + lax.broadcasted_iota(jnp.int32, (tq, nc), 0)
    cidx = lax.broadcasted_iota(jnp.int32, (tq, nc), 1)
    cmask = cidx * CMP_STRIDE + (CMP_BLOCK - 1) <= pos
    gate = jax.nn.sigmoid(gt_ref[...])
    imp = jnp.zeros((tq, LANES), F32)
    outs = []
    for r in range(NSA_GROUP):
        s = lax.dot_general(q[:, r * d:(r + 1) * d], kc, _NT, preferred_element_type=F32)
        s = jnp.where(cmask, s, NEG)
        e = jnp.exp2(s - jnp.max(s, axis=1, keepdims=True))
        p = jnp.where(cmask, e / jnp.sum(e, axis=1, keepdims=True), 0.0)
        pb = p.astype(BF16)
        o = _dot(pb, kvc)[:, d:2 * d]
        outs.append(o * gate[:, 3 * r:3 * r + 1])
        imp = imp + _dot(pb, ovl_ref[...])
    o_ref[...] = jnp.concatenate(outs, axis=1)

    posb = i * tq + lax.broadcasted_iota(jnp.int32, (tq, LANES), 0)
    blk = lax.broadcasted_iota(jnp.int32, (tq, LANES), 1)
    cur = posb >> 6
    forced = jnp.where(blk == 0, 1.0, jnp.where(blk == cur, 1.0, jnp.where(blk == cur - 1, 1.0, 0.0)))
    score = jnp.where(blk * SEL_BLOCK <= posb, imp + forced * FORCE_BONUS, NEG)
    sc = score.T[0:nblk, :]
    grp = 8
    sub = lax.broadcasted_iota(jnp.int32, (grp, tq), 0)
    groups = [sc[g0:g0 + grp, :] for g0 in range(0, nblk, grp)]
    ranks = [jnp.zeros((grp, tq), F32) for _ in groups]
    for sp in range(nblk):
        other = sc[sp:sp + 1, :]
        for gi, mine in enumerate(groups):
            g0 = gi * grp
            ge = jnp.where(other >= mine, 1.0, 0.0)
            gt = jnp.where(other > mine, 1.0, 0.0)
            if g0 > sp:
                beats = ge
            elif g0 + grp - 1 < sp:
                beats = gt
            else:
                beats = jnp.where(sub + g0 > sp, ge, gt)
            ranks[gi] = ranks[gi] + beats
    chosen = jnp.where(jnp.concatenate(ranks, axis=0) < SEL_TOPK, 1.0, 0.0)
    chosen = jnp.concatenate([chosen, jnp.zeros((LANES - nblk, tq), F32)], axis=0).astype(sel_ref.dtype)
    tile = sel_ref.shape[2]
    for t in range(tq // tile):
        sel_ref[t] = chosen[:, t * tile:(t + 1) * tile]


def _cmp_branch(q, kvc, ovl, gates, batch, seq, tq=512, sel_tile=256):
    n = q.shape[0]
    nt = seq // tq
    g = NSA_KV_HEADS
    wq = NSA_GROUP * HEAD_DIM
    nc = kvc.shape[0] // (batch * g)
    kern = functools.partial(_cmp_kernel, nblk=seq // SEL_BLOCK)
    return pl.pallas_call(
        kern,
        grid=(batch, g, nt),
        in_specs=[
            pl.BlockSpec((tq, wq), lambda b, h, i: (b * nt + i, h)),
            pl.BlockSpec((nc, 2 * HEAD_DIM), lambda b, h, i: (b * g + h, 0)),
            pl.BlockSpec(ovl.shape, lambda b, h, i: (0, 0)),
            pl.BlockSpec((tq, LANES), lambda b, h, i: (b * nt + i, h)),
        ],
        out_specs=[
            pl.BlockSpec((tq, wq), lambda b, h, i: (b * nt + i, h)),
            pl.BlockSpec((tq // sel_tile, LANES, sel_tile), lambda b, h, i: ((b * g + h) * nt + i, 0, 0)),
        ],
        out_shape=[
            jax.ShapeDtypeStruct((n, g * wq), F32),
            jax.ShapeDtypeStruct((batch * g * seq // sel_tile, LANES, sel_tile), BF16),
        ],
        compiler_params=_params("parallel", "parallel", "parallel"),
        name="nsa_cmp_select",
    )(q, kvc, ovl, gates)


NSA_STAGES = 3
NSA_CHUNK = 32
MASK_BIG = 1e30


def _nsa_attn_kernel(ti_ref, tj_ref, tm_ref, tf_ref, q_ref, kv_ref, *rest, mode, branch, tq, n_units):
    if mode == "sel":
        (selt_ref, gt_ref, o_ref, qh_ref, kk_ref, vat_ref, ext_ref, bias_ref, s_ref, p_ref, al_ref, m_ref,
         acc_ref, fin_ref) = rest
    else:
        gt_ref, o_ref, qh_ref, kk_ref, vat_ref, bias_ref, s_ref, p_ref, al_ref, m_ref, acc_ref, fin_ref = rest
    tk = tq
    d = HEAD_DIM
    seq = q_ref.shape[0]
    key = lax.broadcasted_iota(jnp.int32, (tk, tq), 0)
    qry = lax.broadcasted_iota(jnp.int32, (tk, tq), 1)
    bias_ref[0] = jnp.zeros((tk, tq), F32)
    bias_ref[1] = jnp.where(key <= qry, 0.0, NEG)
    bias_ref[2] = jnp.where(key > qry, 0.0, NEG)
    s_ref[...] = jnp.zeros_like(s_ref)
    p_ref[...] = jnp.zeros_like(p_ref)
    al_ref[...] = jnp.zeros_like(al_ref)
    acc_ref[...] = jnp.zeros_like(acc_ref)
    m_ref[...] = jnp.full(m_ref.shape, NEG, F32)
    lane = lax.broadcasted_iota(jnp.int32, (tq, 2 * d), 1)
    erow = lax.broadcasted_iota(jnp.int32, (tk, LANES), 0) >> 6
    ecol = lax.broadcasted_iota(jnp.int32, (tk, LANES), 1)

    def prep(c, carry):
        rows = pl.ds(pl.multiple_of(c * tq, tq), tq)
        q = q_ref[rows, :]
        for r in range(NSA_GROUP):
            pair = q[:, (r // 2) * 2 * d:(r // 2 + 1) * 2 * d]
            mine = (lane < d) if r % 2 == 0 else (lane >= d)
            qh_ref[r, rows, :] = jnp.where(mine, pair, 0.0).astype(BF16)
        kv = kv_ref[rows, :]
        k_first = kv[:, 0:d]
        kk_ref[rows, :] = jnp.concatenate([k_first, k_first], axis=1)
        v_first = jnp.concatenate([kv[:, d:2 * d], k_first], axis=1)
        vat_ref[c] = jnp.where(lane < d, v_first, 1.0).astype(F32).T.astype(BF16)
        if mode == "sel":
            ext_ref[c] = jnp.where(ecol == c * (tk // SEL_BLOCK) + erow, 1.0, 0.0).astype(BF16)
        return carry

    lax.fori_loop(0, seq // tq, prep, 0)

    def unit(s, delay):
        u = jnp.clip(s - delay, 0, n_units - 1)
        return ti_ref[u], tj_ref[u], tm_ref[u], tf_ref[u]

    def step(s, carry):
        i_c, j_c, _, f_c = unit(s, 2)
        vat = vat_ref[j_c]
        keep = jnp.where((f_c & 1) == 1, 0.0, 1.0)
        for r in range(NSA_GROUP):
            acc = acc_ref[r] * (al_ref[r] * keep) + _dot(vat, p_ref[r])
            acc_ref[r] = acc
            fin_ref[i_c, r] = acc
        _, _, _, f_b = unit(s, 1)
        first_b = (f_b & 1) == 1
        for r in range(NSA_GROUP):
            m_prev = jnp.where(first_b, NEG, m_ref[r])
            top = None
            for c in range(tk // NSA_CHUNK):
                rows = slice(c * NSA_CHUNK, (c + 1) * NSA_CHUNK)
                part = jnp.max(s_ref[r, rows, :].reshape(NSA_CHUNK // 8, 8, tq), axis=0)
                top = part if top is None else jnp.maximum(top, part)
            m_new = jnp.maximum(m_prev, jnp.max(top, axis=0, keepdims=True))
            al_ref[r] = jnp.exp2(m_prev - m_new)
            m_ref[r] = m_new
            for c in range(tk // NSA_CHUNK):
                rows = slice(c * NSA_CHUNK, (c + 1) * NSA_CHUNK)
                p_ref[r, rows, :] = jnp.exp2(s_ref[r, rows, :] - m_new).astype(BF16)
        i_a, j_a, t_a, _ = unit(s, 0)
        q_rows = pl.ds(pl.multiple_of(i_a * tq, tq), tq)
        if mode == "sel":
            chosen = _dot(ext_ref[j_a], selt_ref[i_a])
            bias_ref[3] = bias_ref[t_a] + (chosen - 1.0) * MASK_BIG
            t_a = 3
        k = kk_ref[pl.ds(pl.multiple_of(j_a * tk, tk), tk), :]
        for r in range(NSA_GROUP):
            s_ref[r] = (lax.dot_general(k, qh_ref[r, q_rows, :], _NT, preferred_element_type=F32)
                        + bias_ref[t_a])
        return carry

    lax.fori_loop(0, n_units + NSA_STAGES - 1, step, 0, unroll=3 if mode == "win" else 4)

    def finish(c, carry):
        o_rows = pl.ds(pl.multiple_of(c * tq, tq), tq)
        gate = jax.nn.sigmoid(gt_ref[o_rows, :]).T
        outs = []
        for r in range(NSA_GROUP):
            acc = fin_ref[c, r]
            outs.append(acc[0:d] / acc[d:d + 1] * gate[3 * r + branch:3 * r + branch + 1])
        o_ref[o_rows, :] = jnp.concatenate(outs, axis=0).T
        return carry

    lax.fori_loop(0, seq // tq, finish, 0)


def _nsa_attn(q, kv, selt, gates, batch, seq, mode, tq=256):
    n = q.shape[0]
    nt = seq // tq
    g = NSA_KV_HEADS
    wq = NSA_GROUP * HEAD_DIM
    branch = 1 if mode == "sel" else 2
    assert WINDOW == 2 * tq
    if mode == "sel":
        units = [(i, j, int(j == i), int(j == 0) | 2 * int(j == i)) for i in range(nt) for j in range(i + 1)]
    else:
        units = [(i, j, (1, 0, 2)[i - j], int(j == max(i - 2, 0)) | 2 * int(j == i))
                 for i in range(nt) for j in range(max(i - 2, 0), i + 1)]
    tables = [jnp.array([u[c] for u in units], jnp.int32) for c in range(4)]
    kern = functools.partial(_nsa_attn_kernel, mode=mode, branch=branch, tq=tq, n_units=len(units))
    in_specs = [
        pl.BlockSpec((seq, wq), lambda b, h, *_: (b, h)),
        pl.BlockSpec((seq, 2 * HEAD_DIM), lambda b, h, *_: (b, h * 3 + branch)),
    ]
    args = [q, kv]
    scratch = [
        pltpu.VMEM((NSA_GROUP, seq, 2 * HEAD_DIM), BF16),
        pltpu.VMEM((seq, 2 * HEAD_DIM), BF16),
        pltpu.VMEM((nt, 2 * HEAD_DIM, tq), BF16),
    ]
    if mode == "sel":
        in_specs.append(pl.BlockSpec((nt, LANES, tq), lambda b, h, *_: (b * g + h, 0, 0)))
        args.append(selt)
        scratch.append(pltpu.VMEM((nt, tq, LANES), BF16))
    in_specs.append(pl.BlockSpec((seq, LANES), lambda b, h, *_: (b, h)))
    args.append(gates)
    scratch += [
        pltpu.VMEM((4, tq, tq), F32),
        pltpu.VMEM((NSA_GROUP, tq, tq), F32),
        pltpu.VMEM((NSA_GROUP, tq, tq), BF16),
        pltpu.VMEM((NSA_GROUP, 1, tq), F32),
        pltpu.VMEM((NSA_GROUP, 1, tq), F32),
        pltpu.VMEM((NSA_GROUP, 2 * HEAD_DIM, tq), F32),
        pltpu.VMEM((nt, NSA_GROUP, 2 * HEAD_DIM, tq), F32),
    ]
    grid_spec = pltpu.PrefetchScalarGridSpec(
        num_scalar_prefetch=4,
        grid=(batch, g),
        in_specs=in_specs,
        out_specs=pl.BlockSpec((seq, wq), lambda b, h, *_: (b, h)),
        scratch_shapes=scratch,
    )
    return pl.pallas_call(
        kern,
        grid_spec=grid_spec,
        out_shape=jax.ShapeDtypeStruct((n, g * wq), F32),
        compiler_params=_params("parallel", "parallel"),
        name="nsa_" + mode,
    )(*tables, *args)


SB_STAGES = 4
SB_CHUNK = 32


def _sb_kernel(ti_ref, tj_ref, q_ref, k_ref, v_ref, o_ref, qh_ref, vt_ref, later_ref, bias_ref,
               z_ref, hl_ref, zl_ref, a_ref, acc_ref, carry_ref, fin_ref, *, tq, n_units):
    tk = tq
    d = HEAD_DIM
    seq = q_ref.shape[0]
    key = lax.broadcasted_iota(jnp.int32, (tk, tq), 0)
    qry = lax.broadcasted_iota(jnp.int32, (tk, tq), 1)
    later_ref[...] = jnp.where(qry > key, 1.0, 0.0).astype(BF16)
    bias_ref[0] = jnp.zeros((tk, tq), F32)
    bias_ref[1] = jnp.where(key < qry, 0.0, NEG)
    z_ref[...] = jnp.zeros_like(z_ref)
    hl_ref[...] = jnp.zeros_like(hl_ref)
    zl_ref[...] = jnp.zeros_like(zl_ref)
    a_ref[...] = jnp.zeros_like(a_ref)
    acc_ref[...] = jnp.zeros_like(acc_ref)
    carry_ref[...] = jnp.zeros_like(carry_ref)

    lane = lax.broadcasted_iota(jnp.int32, (tq, 2 * d), 1)

    def split(c, carry):
        rows = pl.ds(pl.multiple_of(c * tq, tq), tq)
        q = q_ref[rows, :]
        qh_ref[0, rows, :] = jnp.where(lane < d, q, 0.0).astype(BF16)
        qh_ref[1, rows, :] = jnp.where(lane >= d, q, 0.0).astype(BF16)
        vt_ref[c] = v_ref[rows, :].astype(F32).T.astype(BF16)
        return carry

    lax.fori_loop(0, seq // tq, split, 0)
    feat = lax.broadcasted_iota(jnp.int32, (2 * d, tq), 0)

    def unit(s, delay):
        u = jnp.clip(s - delay, 0, n_units - 1)
        return ti_ref[u], tj_ref[u]

    def step(s, carry):
        i_e, j_e = unit(s, 3)
        vt = vt_ref[j_e]
        keep = jnp.where(i_e == j_e, 0.0, 1.0)
        accs = []
        for h in range(2):
            acc = acc_ref[h] * keep + _dot(vt, a_ref[h])
            acc_ref[h] = acc
            accs.append(acc)
        fin_ref[i_e] = jnp.where(feat < d, accs[0], accs[1])
        for h in range(2):
            between = _dot(later_ref[...], hl_ref[h])
            for c in range(tk // SB_CHUNK):
                rows = slice(c * SB_CHUNK, (c + 1) * SB_CHUNK)
                a_ref[h, rows, :] = jnp.exp2(zl_ref[h, rows, :] + between[rows]).astype(BF16)
        i_b, j_b = unit(s, 1)
        keep_b = jnp.where(i_b == j_b, 0.0, 1.0)
        for h in range(2):
            before = carry_ref[h] * keep_b
            part = jnp.zeros((8, tq), F32)
            for c in range(tk // SB_CHUNK):
                rows = slice(c * SB_CHUNK, (c + 1) * SB_CHUNK)
                z = z_ref[h, rows, :]
                nz = -z
                log_stay = jnp.minimum(nz, 0.0) - jnp.log(1.0 + jnp.exp2(jnp.minimum(z, nz))) * LOG2_E
                hl_ref[h, rows, :] = log_stay.astype(BF16)
                zl_ref[h, rows, :] = z + log_stay + before
                part = part + jnp.sum(log_stay.reshape(SB_CHUNK // 8, 8, tq), axis=0)
            carry_ref[h] = before + jnp.sum(part, axis=0, keepdims=True)
        i_a, j_a = unit(s, 0)
        bias = bias_ref[jnp.where(i_a == j_a, 1, 0)]
        q_rows = pl.ds(pl.multiple_of(i_a * tq, tq), tq)
        k_rows = pl.ds(pl.multiple_of(j_a * tk, tk), tk)
        for h in range(2):
            z_ref[h] = lax.dot_general(k_ref[k_rows, :], qh_ref[h, q_rows, :], _NT,
                                       preferred_element_type=F32) + bias
        return carry

    lax.fori_loop(0, n_units + SB_STAGES - 1, step, 0, unroll=4)

    def finish(c, carry):
        o_ref[pl.ds(pl.multiple_of(c * tq, tq), tq), :] = fin_ref[c].T.astype(o_ref.dtype)
        return carry

    lax.fori_loop(0, seq // tq, finish, 0)


def _stick_breaking(qkv, batch, seq, heads, tq=256):
    n = qkv.shape[0]
    nt = seq // tq
    pairs = heads // 2
    units = [(i, j) for i in range(nt) for j in range(i, -1, -1)]
    ti = jnp.array([u[0] for u in units], jnp.int32)
    tj = jnp.array([u[1] for u in units], jnp.int32)
    kern = functools.partial(_sb_kernel, tq=tq, n_units=len(units))
    grid_spec = pltpu.PrefetchScalarGridSpec(
        num_scalar_prefetch=2,
        grid=(batch, pairs),
        in_specs=[
            pl.BlockSpec((seq, LANES), lambda b, h, ti, tj: (b, h)),
            pl.BlockSpec((seq, LANES), lambda b, h, ti, tj: (b, pairs + h)),
            pl.BlockSpec((seq, LANES), lambda b, h, ti, tj: (b, 2 * pairs + h)),
        ],
        out_specs=pl.BlockSpec((seq, LANES), lambda b, h, ti, tj: (b, h)),
        scratch_shapes=[
            pltpu.VMEM((2, seq, 2 * HEAD_DIM), BF16),
            pltpu.VMEM((nt, 2 * HEAD_DIM, tq), BF16),
            pltpu.VMEM((tq, tq), BF16),
            pltpu.VMEM((2, tq, tq), F32),
            pltpu.VMEM((2, tq, tq), F32),
            pltpu.VMEM((2, tq, tq), BF16),
            pltpu.VMEM((2, tq, tq), F32),
            pltpu.VMEM((2, tq, tq), BF16),
            pltpu.VMEM((2, 2 * HEAD_DIM, tq), F32),
            pltpu.VMEM((2, 1, tq), F32),
            pltpu.VMEM((nt, 2 * HEAD_DIM, tq), F32),
        ],
    )
    return pl.pallas_call(
        kern,
        grid_spec=grid_spec,
        out_shape=jax.ShapeDtypeStruct((n, heads * HEAD_DIM), BF16),
        compiler_params=_params("parallel", "parallel"),
        name="stick_breaking",
    )(ti, tj, qkv, qkv, qkv)


def _out_even_kernel(x_ref, yc_ref, oc_ref, os_ref, ow_ref, w_ref, o_ref):
    cd = yc_ref.shape[1]
    y_nsa = (oc_ref[...] + os_ref[...] + ow_ref[...]).astype(BF16)
    o_ref[...] = x_ref[...] + _dot(yc_ref[...], w_ref[0:cd, :]) + _dot(y_nsa, w_ref[cd:, :])


def _out_odd_kernel(x_ref, y_ref, w_ref, o_ref):
    o_ref[...] = x_ref[...] + _dot(y_ref[...], w_ref[...])


def _out_proj(kern, x, ys, w, tm=512):
    n, d = x.shape
    return pl.pallas_call(
        kern,
        grid=(n // tm,),
        in_specs=[pl.BlockSpec((tm, d), lambda i: (i, 0))]
        + [pl.BlockSpec((tm, y.shape[1]), lambda i: (i, 0)) for y in ys]
        + [pl.BlockSpec(w.shape, lambda i: (0, 0))],
        out_specs=pl.BlockSpec((tm, d), lambda i: (i, 0)),
        out_shape=jax.ShapeDtypeStruct((n, d), F32),
        compiler_params=_params("parallel"),
        name="out_proj",
    )(x, *ys, w)


def _final_norm_kernel(x_ref, g_ref, o_ref):
    o_ref[...] = _rms(x_ref[...], g_ref[...])


def _final_norm(x, g, tm=512):
    n, d = x.shape
    return pl.pallas_call(
        _final_norm_kernel,
        grid=(n // tm,),
        in_specs=[pl.BlockSpec((tm, d), lambda i: (i, 0)), pl.BlockSpec((1, d), lambda i: (0, 0))],
        out_specs=pl.BlockSpec((tm, d), lambda i: (i, 0)),
        out_shape=jax.ShapeDtypeStruct((n, d), F32),
        compiler_params=_params("parallel"),
        name="final_norm",
    )(x, g.reshape(1, d))


def _overlap_matrix(nch, nblk):
    c0 = jnp.arange(nch) * CMP_STRIDE
    s0 = jnp.arange(LANES) * SEL_BLOCK
    lo = jnp.maximum(c0[:, None], s0[None, :])
    hi = jnp.minimum(c0[:, None] + CMP_BLOCK, s0[None, :] + SEL_BLOCK)
    ovl = jnp.maximum(hi - lo, 0).astype(F32) / CMP_BLOCK
    keep = (jnp.arange(nch)[:, None] < nch - 1) & (jnp.arange(LANES)[None, :] < nblk)
    return jnp.where(keep, ovl, 0.0).astype(BF16)


def _conv_nsa_mixer(x, g, w_in, conv_w, pe_k, w1_k, w2_k, pe_v, w1_v, w2_v, w_out, batch, seq):
    d = HEAD_DIM
    cd = conv_w.shape[1]
    kvd = NSA_KV_HEADS * d
    o_q = 3 * cd
    o_kv = o_q + NSA_KV_HEADS * NSA_GROUP * d
    o_g = o_kv + 6 * kvd
    ng = NSA_GROUP * 3
    kv_cols = []
    gate_cols = []
    for h in range(NSA_KV_HEADS):
        for typ in range(3):
            base = o_kv + typ * 2 * kvd + h * d
            kv_cols += [w_in[:, base:base + d], w_in[:, base + kvd:base + kvd + d]]
        gate_cols += [w_in[:, o_g + h * ng:o_g + (h + 1) * ng], jnp.zeros((w_in.shape[0], LANES - ng), F32)]
    w_all = jnp.concatenate(
        [w_in[:, :o_q], w_in[:, o_q:o_kv] * (Q_SCALE * LOG2_E)] + kv_cols + gate_cols, axis=1).astype(BF16)
    cv, q, kv, gates = _norm_proj(
        x, g, w_all,
        [(3 * cd, F32), (o_kv - o_q, BF16), (6 * kvd, BF16), (NSA_KV_HEADS * LANES, F32)])

    y_conv = _short_conv(cv, conv_w, seq)

    nch = seq // CMP_STRIDE
    kv6 = kv.reshape(batch, seq, NSA_KV_HEADS, 3, 2, d)
    cm = kv6[:, :, :, 0].transpose(0, 2, 3, 1, 4).reshape(batch * NSA_KV_HEADS * 2 * nch, CMP_STRIDE * d)
    w1 = jnp.stack([w1_k, w1_v]).astype(BF16)
    w2 = jnp.stack([w2_k, w2_v]).astype(BF16)
    pe = jnp.stack([pe_k.reshape(1, -1), pe_v.reshape(1, -1)])
    kvc = _compress(cm, w1, w2, pe, nch)

    ovl = _overlap_matrix(nch, seq // SEL_BLOCK)
    o_cmp, sel = _cmp_branch(q, kvc, ovl, gates, batch, seq)
    o_sel = _nsa_attn(q, kv, sel, gates, batch, seq, "sel")
    o_win = _nsa_attn(q, kv, sel, gates, batch, seq, "win")
    return _out_proj(_out_even_kernel, x, [y_conv, o_cmp, o_sel, o_win], w_out.astype(BF16))


def _stick_breaking_mixer(x, g, w_qkv, w_out, batch, seq):
    hd = w_out.shape[0]
    w = jnp.concatenate([w_qkv[:, :hd] * (Q_SCALE * LOG2_E), w_qkv[:, hd:]], axis=1).astype(BF16)
    (qkv,) = _norm_proj(x, g, w, [(3 * hd, BF16)])
    y = _stick_breaking(qkv, batch, seq, hd // HEAD_DIM)
    return _out_proj(_out_odd_kernel, x, [y], w_out.astype(BF16))


def kernel(x, norm_ffn1, w_ffn1_in, w_ffn1_out, norm_mix, w_in_ab, conv_w, cmp_pe_k, cmp_w1_k, cmp_w2_k,
           cmp_pe_v, cmp_w1_v, cmp_w2_v, w_out_ab, w_qkv_sb, w_out_sb, norm_ffn2, w_ffn2_in, w_ffn2_out,
           norm_final):
    batch, seq, d_model = x.shape
    depth = norm_ffn1.shape[0]
    x = x.reshape(batch * seq, d_model)
    for layer in range(depth):
        x = _ffn(x, norm_ffn1[layer], w_ffn1_in[layer].astype(BF16), w_ffn1_out[layer].astype(BF16))
        i = layer // 2
        if layer % 2 == 0:
            x = _conv_nsa_mixer(x, norm_mix[layer], w_in_ab[i], conv_w[i], cmp_pe_k[i], cmp_w1_k[i],
                                cmp_w2_k[i], cmp_pe_v[i], cmp_w1_v[i], cmp_w2_v[i], w_out_ab[i], batch, seq)
        else:
            x = _stick_breaking_mixer(x, norm_mix[layer], w_qkv_sb[i], w_out_sb[i], batch, seq)
        x = _ffn(x, norm_ffn2[layer], w_ffn2_in[layer].astype(BF16), w_ffn2_out[layer].astype(BF16))
    return _final_norm(x, norm_final).reshape(batch, seq, d_model)
```

```python
import functools

import jax
import jax.numpy as jnp
from jax import lax
from jax.experimental import pallas as pl
from jax.experimental.pallas import tpu as pltpu

F32 = jnp.float32
BF16 = jnp.bfloat16

EPS = 1e-6
NEG = -1e30
HEAD_DIM = 64
Q_SCALE = HEAD_DIM ** -0.5
LOG2_E = 1.4426950408889634
CONV_WIDTH = 3
NSA_KV_HEADS = 2
NSA_GROUP = 4
CMP_BLOCK = 32
CMP_STRIDE = 16
SEL_BLOCK = 64
SEL_TOPK = 16
WINDOW = 512
FORCE_BONUS = 1e4
LANES = 128
VMEM_LIMIT = 56 * 1024 * 1024

_NT = (((1,), (1,)), ((), ()))


def _params(*sem, flags=None):
    return pltpu.CompilerParams(dimension_semantics=sem, vmem_limit_bytes=VMEM_LIMIT, flags=flags)


def _rms(x, g):
    ms = jnp.mean(x * x, axis=-1, keepdims=True)
    return x * lax.rsqrt(ms + EPS) * g


def _dot(a, b):
    return jnp.dot(a, b, preferred_element_type=F32)


def _ffn_kernel(x_ref, g_ref, wi_ref, wo_ref, *rest, tf, n_mix, final):
    o_ref = rest[-1]
    f = wo_ref.shape[0]
    x = x_ref[...]
    if n_mix:
        ys, wm_ref = rest[:n_mix], rest[n_mix]
        rows = ys[0].shape[1]
        x = x + _dot(ys[0][...], wm_ref[0:rows, :])
        if n_mix > 1:
            y_sum = ys[1][...]
            for y_ref in ys[2:]:
                y_sum = y_sum + y_ref[...]
            x = x + _dot(y_sum.astype(BF16), wm_ref[rows:, :])
    h = _rms(x, g_ref[...]).astype(BF16)
    acc = None
    for c in range(f // tf):
        gate = _dot(h, wi_ref[:, c * tf:(c + 1) * tf])
        up = _dot(h, wi_ref[:, f + c * tf:f + (c + 1) * tf])
        act = gate * jax.nn.sigmoid(gate) * up
        part = _dot(act.astype(BF16), wo_ref[c * tf:(c + 1) * tf, :])
        acc = part if acc is None else acc + part
    out = x + 0.5 * acc
    if final:
        out = _rms(out, rest[-2][...])
    o_ref[...] = out


def _ffn(x, g, w_in, w_out, mix=None, final_gain=None, tm=512):
    n, d = x.shape
    f = w_out.shape[0]
    tf = f // 2 if (f // 2) % LANES == 0 else f
    resident = pl.Buffered(1)
    in_specs = [
        pl.BlockSpec((tm, d), lambda i: (i, 0)),
        pl.BlockSpec((1, d), lambda i: (0, 0)),
        pl.BlockSpec(w_in.shape, lambda i: (0, 0), pipeline_mode=resident),
        pl.BlockSpec(w_out.shape, lambda i: (0, 0), pipeline_mode=resident),
    ]
    args = [x, g.reshape(1, d), w_in, w_out]
    n_mix = 0
    if mix is not None:
        ys, w_mix = mix
        n_mix = len(ys)
        in_specs += [pl.BlockSpec((tm, y.shape[1]), lambda i: (i, 0)) for y in ys]
        in_specs.append(pl.BlockSpec(w_mix.shape, lambda i: (0, 0), pipeline_mode=resident))
        args += [*ys, w_mix]
    if final_gain is not None:
        in_specs.append(pl.BlockSpec((1, d), lambda i: (0, 0)))
        args.append(final_gain.reshape(1, d))
    return pl.pallas_call(
        functools.partial(_ffn_kernel, tf=tf, n_mix=n_mix, final=final_gain is not None),
        grid=(n // tm,),
        in_specs=in_specs,
        out_specs=pl.BlockSpec((tm, d), lambda i: (i, 0)),
        out_shape=jax.ShapeDtypeStruct((n, d), F32),
        compiler_params=_params("parallel"),
        name="ffn",
    )(*args)


def _proj_kernel(x_ref, g_ref, w_ref, *o_refs):
    h = _rms(x_ref[...], g_ref[...]).astype(BF16)
    off = 0
    for o_ref in o_refs:
        width = o_ref.shape[1]
        o_ref[...] = _dot(h, w_ref[:, off:off + width]).astype(o_ref.dtype)
        off += width


def _norm_proj(x, g, w, outs, tm=512):
    n, d = x.shape
    return pl.pallas_call(
        _proj_kernel,
        grid=(n // tm,),
        in_specs=[
            pl.BlockSpec((tm, d), lambda i: (i, 0)),
            pl.BlockSpec((1, d), lambda i: (0, 0)),
            pl.BlockSpec(w.shape, lambda i: (0, 0)),
        ],
        out_specs=[pl.BlockSpec((tm, wd), lambda i: (i, 0)) for wd, _ in outs],
        out_shape=[jax.ShapeDtypeStruct((n, wd), dt) for wd, dt in outs],
        compiler_params=_params("parallel"),
        name="norm_proj",
    )(x, g.reshape(1, d), w)


HALO = 8


def _proj_conv_kernel(x_ref, prev_ref, g_ref, w_ref, cw_ref, yc_ref, *rest, tiles_per_seq, cd):
    o_refs, ext_ref = rest[:-1], rest[-1]
    tm = x_ref.shape[0]
    first = (pl.program_id(0) % tiles_per_seq) == 0
    x_ext = jnp.concatenate([prev_ref[...], x_ref[...]], axis=0)
    h = _rms(x_ext, g_ref[...]).astype(BF16)
    cv = _dot(h, w_ref[:, 0:3 * cd])
    u = cv[:, cd:2 * cd] * cv[:, 2 * cd:3 * cd]
    ext_ref[...] = u
    ext_ref[0:HALO, :] = jnp.where(first, 0.0, u[0:HALO])
    w = cw_ref[...]
    y = (w[0:1] * ext_ref[HALO - 2:HALO - 2 + tm, :] + w[1:2] * ext_ref[HALO - 1:HALO - 1 + tm, :]
         + w[2:3] * u[HALO:])
    yc_ref[...] = (cv[HALO:, 0:cd] * y).astype(yc_ref.dtype)
    off = 3 * cd
    for o_ref in o_refs:
        width = o_ref.shape[1]
        o_ref[...] = _dot(h, w_ref[:, off:off + width])[HALO:].astype(o_ref.dtype)
        off += width


def _proj_conv(x, g, w, conv_w, outs, seq, tm=512):
    n, d = x.shape
    cd = conv_w.shape[1]
    kern = functools.partial(_proj_conv_kernel, tiles_per_seq=seq // tm, cd=cd)
    return pl.pallas_call(
        kern,
        grid=(n // tm,),
        in_specs=[
            pl.BlockSpec((tm, d), lambda i: (i, 0)),
            pl.BlockSpec((HALO, d), lambda i: (jnp.maximum(i * (tm // HALO) - 1, 0), 0)),
            pl.BlockSpec((1, d), lambda i: (0, 0)),
            pl.BlockSpec(w.shape, lambda i: (0, 0)),
            pl.BlockSpec(conv_w.shape, lambda i: (0, 0)),
        ],
        out_specs=[pl.BlockSpec((tm, cd), lambda i: (i, 0))]
        + [pl.BlockSpec((tm, wd), lambda i: (i, 0)) for wd, _ in outs],
        out_shape=[jax.ShapeDtypeStruct((n, cd), BF16)]
        + [jax.ShapeDtypeStruct((n, wd), dt) for wd, dt in outs],
        scratch_shapes=[pltpu.VMEM((tm + HALO, cd), F32)],
        compiler_params=_params("parallel"),
        name="proj_conv",
    )(x, x, g.reshape(1, d), w, conv_w)


def _compress_kernel(cm_ref, w1_ref, w2_ref, pe_ref, o_ref):
    nch = o_ref.shape[0]
    half = cm_ref.shape[1]
    outs = []
    for t in range(2):
        cm = cm_ref[t * nch:(t + 1) * nch, :]
        a = _dot(cm, w1_ref[t, 0:half, :])
        b = _dot(cm, w1_ref[t, half:2 * half, :])
        pe = jnp.broadcast_to(pe_ref[t], (8, 2 * half)).astype(BF16)
        bias = _dot(pe, w1_ref[t])[0:1]
        hid = a + pltpu.roll(b, nch - 1, 0) + bias
        outs.append(_dot(jax.nn.gelu(hid).astype(BF16), w2_ref[t]))
    o_ref[...] = jnp.concatenate(outs, axis=1).astype(o_ref.dtype)


def _compress(cm, w1, w2, pe, nch):
    rows, half = cm.shape
    bg = rows // (2 * nch)
    d = w2.shape[2]
    return pl.pallas_call(
        _compress_kernel,
        grid=(bg,),
        in_specs=[
            pl.BlockSpec((2 * nch, half), lambda i: (i, 0)),
            pl.BlockSpec(w1.shape, lambda i: (0, 0, 0)),
            pl.BlockSpec(w2.shape, lambda i: (0, 0, 0)),
            pl.BlockSpec(pe.shape, lambda i: (0, 0, 0)),
        ],
        out_specs=pl.BlockSpec((nch, 2 * d), lambda i: (i, 0)),
        out_shape=jax.ShapeDtypeStruct((bg * nch, 2 * d), BF16),
        compiler_params=_params("parallel"),
        name="compress",
    )(cm, w1, w2, pe)


def _cmp_kernel(q_ref, kvc_ref, ovl_ref, gt_ref, o_ref, sel_ref, *, nblk):
    tq = q_ref.shape[0]
    nc = kvc_ref.shape[0]
    d = HEAD_DIM
    i = pl.program_id(2)
    q = q_ref[...]
    kvc = kvc_ref[...]
    kc = kvc[:, 0:d]
    pos = i * tq + lax.broadcasted_iota(jnp.int32, (tq, nc), 0)
    cidx = lax.broadcasted_iota(jnp.int32, (tq, nc), 1)
    cmask = cidx * CMP_STRIDE + (CMP_BLOCK - 1) <= pos
    gate = jax.nn.sigmoid(gt_ref[...])
    imp = jnp.zeros((tq, LANES), F32)
    outs = []
    for r in range(NSA_GROUP):
        s = lax.dot_general(q[:, r * d:(r + 1) * d], kc, _NT, preferred_element_type=F32)
        s = jnp.where(cmask, s, NEG)
        e = jnp.exp2(s - jnp.max(s, axis=1, keepdims=True))
        p = jnp.where(cmask, e / jnp.sum(e, axis=1, keepdims=True), 0.0)
        pb = p.astype(BF16)
        o = _dot(pb, kvc)[:, d:2 * d]
        outs.append(o * gate[:, 3 * r:3 * r + 1])
        imp = imp + _dot(pb, ovl_ref[...])
    o_ref[...] = jnp.concatenate(outs, axis=1)

    posb = i * tq + lax.broadcasted_iota(jnp.int32, (tq, LANES), 0)
    blk = lax.broadcasted_iota(jnp.int32, (tq, LANES), 1)
    cur = posb >> 6
    forced = jnp.where(blk == 0, 1.0, jnp.where(blk == cur, 1.0, jnp.where(blk == cur - 1, 1.0, 0.0)))
    score = jnp.where(blk * SEL_BLOCK <= posb, imp + forced * FORCE_BONUS, NEG)
    sc = score.T[0:nblk, :]
    grp = 8
    sub = lax.broadcasted_iota(jnp.int32, (grp, tq), 0)
    groups = [sc[g0:g0 + grp, :] for g0 in range(0, nblk, grp)]
    ranks = [jnp.zeros((grp, tq), F32) for _ in groups]
    for sp in range(nblk):
        other = sc[sp:sp + 1, :]
        for gi, mine in enumerate(groups):
            g0 = gi * grp
            ge = jnp.where(other >= mine, 1.0, 0.0)
            gt = jnp.where(other > mine, 1.0, 0.0)
            if g0 > sp:
                beats = ge
            elif g0 + grp - 1 < sp:
                beats = gt
            else:
                beats = jnp.where(sub + g0 > sp, ge, gt)
            ranks[gi] = ranks[gi] + beats
    chosen = jnp.where(jnp.concatenate(ranks, axis=0) < SEL_TOPK, 1.0, 0.0)
    chosen = jnp.concatenate([chosen, jnp.zeros((LANES - nblk, tq), F32)], axis=0).astype(sel_ref.dtype)
    tile = sel_ref.shape[2]
    for t in range(tq // tile):
        sel_ref[t] = chosen[:, t * tile:(t + 1) * tile]


def _cmp_branch(q, kvc, ovl, gates, batch, seq, tq=512, sel_tile=256):
    n = q.shape[0]
    nt = seq // tq
    g = NSA_KV_HEADS
    wq = NSA_GROUP * HEAD_DIM
    nc = kvc.shape[0] // (batch * g)
    kern = functools.partial(_cmp_kernel, nblk=seq // SEL_BLOCK)
    return pl.pallas_call(
        kern,
        grid=(batch, g, nt),
        in_specs=[
            pl.BlockSpec((tq, wq), lambda b, h, i: (b * nt + i, h)),
            pl.BlockSpec((nc, 2 * HEAD_DIM), lambda b, h, i: (b * g + h, 0)),
            pl.BlockSpec(ovl.shape, lambda b, h, i: (0, 0)),
            pl.BlockSpec((tq, LANES), lambda b, h, i: (b * nt + i, h)),
        ],
        out_specs=[
            pl.BlockSpec((tq, wq), lambda b, h, i: (b * nt + i, h)),
            pl.BlockSpec((tq // sel_tile, LANES, sel_tile), lambda b, h, i: ((b * g + h) * nt + i, 0, 0)),
        ],
        out_shape=[
            jax.ShapeDtypeStruct((n, g * wq), F32),
            jax.ShapeDtypeStruct((batch * g * seq // sel_tile, LANES, sel_tile), BF16),
        ],
        compiler_params=_params("parallel", "parallel", "parallel"),
        name="nsa_cmp_select",
    )(q, kvc, ovl, gates)


NSA_STAGES = 3
NSA_CHUNK = 32
MASK_BIG = 1e30


def _nsa_attn_kernel(ti_ref, tj_ref, tm_ref, tf_ref, q_ref, kv_ref, *rest, mode, branch, tq, n_units):
    if mode == "sel":
        (selt_ref, gt_ref, o_ref, qh_ref, kk_ref, vat_ref, ext_ref, bias_ref, s_ref, p_ref, al_ref, m_ref,
         acc_ref, fin_ref) = rest
    else:
        gt_ref, o_ref, qh_ref, kk_ref, vat_ref, bias_ref, s_ref, p_ref, al_ref, m_ref, acc_ref, fin_ref = rest
    tk = tq
    d = HEAD_DIM
    seq = q_ref.shape[0]
    key = lax.broadcasted_iota(jnp.int32, (tk, tq), 0)
    qry = lax.broadcasted_iota(jnp.int32, (tk, tq), 1)
    bias_ref[0] = jnp.zeros((tk, tq), F32)
    bias_ref[1] = jnp.where(key <= qry, 0.0, NEG)
    bias_ref[2] = jnp.where(key > qry, 0.0, NEG)
    s_ref[...] = jnp.zeros_like(s_ref)
    p_ref[...] = jnp.zeros_like(p_ref)
    al_ref[...] = jnp.zeros_like(al_ref)
    acc_ref[...] = jnp.zeros_like(acc_ref)
    m_ref[...] = jnp.full(m_ref.shape, NEG, F32)
    lane = lax.broadcasted_iota(jnp.int32, (tq, 2 * d), 1)
    erow = lax.broadcasted_iota(jnp.int32, (tk, LANES), 0) >> 6
    ecol = lax.broadcasted_iota(jnp.int32, (tk, LANES), 1)

    def prep(c, carry):
        rows = pl.ds(pl.multiple_of(c * tq, tq), tq)
        q = q_ref[rows, :]
        for r in range(NSA_GROUP):
            pair = q[:, (r // 2) * 2 * d:(r // 2 + 1) * 2 * d]
            mine = (lane < d) if r % 2 == 0 else (lane >= d)
            qh_ref[r, rows, :] = jnp.where(mine, pair, 0.0).astype(BF16)
        kv = kv_ref[rows, :]
        k_first = kv[:, 0:d]
        kk_ref[rows, :] = jnp.concatenate([k_first, k_first], axis=1)
        v_first = jnp.concatenate([kv[:, d:2 * d], k_first], axis=1)
        vat_ref[c] = jnp.where(lane < d, v_first, 1.0).astype(F32).T.astype(BF16)
        if mode == "sel":
            ext_ref[c] = jnp.where(ecol == c * (tk // SEL_BLOCK) + erow, 1.0, 0.0).astype(BF16)
        return carry

    lax.fori_loop(0, seq // tq, prep, 0)

    def unit(s, delay):
        u = jnp.clip(s - delay, 0, n_units - 1)
        return ti_ref[u], tj_ref[u], tm_ref[u], tf_ref[u]

    def step(s, carry):
        i_c, j_c, _, f_c = unit(s, 2)
        vat = vat_ref[j_c]
        keep = jnp.where((f_c & 1) == 1, 0.0, 1.0)
        for r in range(NSA_GROUP):
            acc = acc_ref[r] * (al_ref[r] * keep) + _dot(vat, p_ref[r])
            acc_ref[r] = acc
            fin_ref[i_c, r] = acc
        _, _, _, f_b = unit(s, 1)
        first_b = (f_b & 1) == 1
        for r in range(NSA_GROUP):
            m_prev = jnp.where(first_b, NEG, m_ref[r])
            top = None
            for c in range(tk // NSA_CHUNK):
                rows = slice(c * NSA_CHUNK, (c + 1) * NSA_CHUNK)
                part = jnp.max(s_ref[r, rows, :].reshape(NSA_CHUNK // 8, 8, tq), axis=0)
                top = part if top is None else jnp.maximum(top, part)
            m_new = jnp.maximum(m_prev, jnp.max(top, axis=0, keepdims=True))
            al_ref[r] = jnp.exp2(m_prev - m_new)
            m_ref[r] = m_new
            for c in range(tk // NSA_CHUNK):
                rows = slice(c * NSA_CHUNK, (c + 1) * NSA_CHUNK)
                p_ref[r, rows, :] = jnp.exp2(s_ref[r, rows, :] - m_new).astype(BF16)
        i_a, j_a, t_a, _ = unit(s, 0)
        q_rows = pl.ds(pl.multiple_of(i_a * tq, tq), tq)
        if mode == "sel":
            chosen = _dot(ext_ref[j_a], selt_ref[i_a])
            bias_ref[3] = bias_ref[t_a] + (chosen - 1.0) * MASK_BIG
            t_a = 3
        k = kk_ref[pl.ds(pl.multiple_of(j_a * tk, tk), tk), :]
        for r in range(NSA_GROUP):
            s_ref[r] = (lax.dot_general(k, qh_ref[r, q_rows, :], _NT, preferred_element_type=F32)
                        + bias_ref[t_a])
        return carry

    lax.fori_loop(0, n_units + NSA_STAGES - 1, step, 0, unroll=3 if mode == "win" else 4)

    def finish(c, carry):
        o_rows = pl.ds(pl.multiple_of(c * tq, tq), tq)
        gate = jax.nn.sigmoid(gt_ref[o_rows, :]).T
        outs = []
        for r in range(NSA_GROUP):
            acc = fin_ref[c, r]
            outs.append(acc[0:d] / acc[d:d + 1] * gate[3 * r + branch:3 * r + branch + 1])
        o_ref[o_rows, :] = jnp.concatenate(outs, axis=0).T
        return carry

    lax.fori_loop(0, seq // tq, finish, 0)


def _nsa_attn(q, kv, selt, gates, batch, seq, mode, tq=256):
    n = q.shape[0]
    nt = seq // tq
    g = NSA_KV_HEADS
    wq = NSA_GROUP * HEAD_DIM
    branch = 1 if mode == "sel" else 2
    assert WINDOW == 2 * tq
    if mode == "sel":
        units = [(i, j, int(j == i), int(j == 0) | 2 * int(j == i)) for i in range(nt) for j in range(i + 1)]
    else:
        units = [(i, j, (1, 0, 2)[i - j], int(j == max(i - 2, 0)) | 2 * int(j == i))
                 for i in range(nt) for j in range(max(i - 2, 0), i + 1)]
    tables = [jnp.array([u[c] for u in units], jnp.int32) for c in range(4)]
    kern = functools.partial(_nsa_attn_kernel, mode=mode, branch=branch, tq=tq, n_units=len(units))
    in_specs = [
        pl.BlockSpec((seq, wq), lambda b, h, *_: (b, h)),
        pl.BlockSpec((seq, 2 * HEAD_DIM), lambda b, h, *_: (b, h * 3 + branch)),
    ]
    args = [q, kv]
    scratch = [
        pltpu.VMEM((NSA_GROUP, seq, 2 * HEAD_DIM), BF16),
        pltpu.VMEM((seq, 2 * HEAD_DIM), BF16),
        pltpu.VMEM((nt, 2 * HEAD_DIM, tq), BF16),
    ]
    if mode == "sel":
        in_specs.append(pl.BlockSpec((nt, LANES, tq), lambda b, h, *_: (b * g + h, 0, 0)))
        args.append(selt)
        scratch.append(pltpu.VMEM((nt, tq, LANES), BF16))
    in_specs.append(pl.BlockSpec((seq, LANES), lambda b, h, *_: (b, h)))
    args.append(gates)
    scratch += [
        pltpu.VMEM((4, tq, tq), F32),
        pltpu.VMEM((NSA_GROUP, tq, tq), F32),
        pltpu.VMEM((NSA_GROUP, tq, tq), BF16),
        pltpu.VMEM((NSA_GROUP, 1, tq), F32),
        pltpu.VMEM((NSA_GROUP, 1, tq), F32),
        pltpu.VMEM((NSA_GROUP, 2 * HEAD_DIM, tq), F32),
        pltpu.VMEM((nt, NSA_GROUP, 2 * HEAD_DIM, tq), F32),
    ]
    grid_spec = pltpu.PrefetchScalarGridSpec(
        num_scalar_prefetch=4,
        grid=(batch, g),
        in_specs=in_specs,
        out_specs=pl.BlockSpec((seq, wq), lambda b, h, *_: (b, h)),
        scratch_shapes=scratch,
    )
    return pl.pallas_call(
        kern,
        grid_spec=grid_spec,
        out_shape=jax.ShapeDtypeStruct((n, g * wq), F32),
        compiler_params=_params("parallel", "parallel"),
        name="nsa_" + mode,
    )(*tables, *args)


SB_STAGES = 4
SB_CHUNK = 32


def _sb_kernel(ti_ref, tj_ref, q_ref, k_ref, v_ref, o_ref, qh_ref, vt_ref, later_ref, bias_ref,
               z_ref, hl_ref, zl_ref, a_ref, acc_ref, carry_ref, fin_ref, *, tq, n_units):
    tk = tq
    d = HEAD_DIM
    seq = q_ref.shape[0]
    key = lax.broadcasted_iota(jnp.int32, (tk, tq), 0)
    qry = lax.broadcasted_iota(jnp.int32, (tk, tq), 1)
    later_ref[...] = jnp.where(qry > key, 1.0, 0.0).astype(BF16)
    bias_ref[0] = jnp.zeros((tk, tq), F32)
    bias_ref[1] = jnp.where(key < qry, 0.0, NEG)
    z_ref[...] = jnp.zeros_like(z_ref)
    hl_ref[...] = jnp.zeros_like(hl_ref)
    zl_ref[...] = jnp.zeros_like(zl_ref)
    a_ref[...] = jnp.zeros_like(a_ref)
    acc_ref[...] = jnp.zeros_like(acc_ref)
    carry_ref[...] = jnp.zeros_like(carry_ref)

    lane = lax.broadcasted_iota(jnp.int32, (tq, 2 * d), 1)

    def split(c, carry):
        rows = pl.ds(pl.multiple_of(c * tq, tq), tq)
        q = q_ref[rows, :]
        qh_ref[0, rows, :] = jnp.where(lane < d, q, 0.0).astype(BF16)
        qh_ref[1, rows, :] = jnp.where(lane >= d, q, 0.0).astype(BF16)
        vt_ref[c] = v_ref[rows, :].astype(F32).T.astype(BF16)
        return carry

    lax.fori_loop(0, seq // tq, split, 0)
    feat = lax.broadcasted_iota(jnp.int32, (2 * d, tq), 0)

    def unit(s, delay):
        u = jnp.clip(s - delay, 0, n_units - 1)
        return ti_ref[u], tj_ref[u]

    def step(s, carry):
        i_e, j_e = unit(s, 3)
        vt = vt_ref[j_e]
        keep = jnp.where(i_e == j_e, 0.0, 1.0)
        accs = []
        for h in range(2):
            acc = acc_ref[h] * keep + _dot(vt, a_ref[h])
            acc_ref[h] = acc
            accs.append(acc)
        fin_ref[i_e] = jnp.where(feat < d, accs[0], accs[1])
        for h in range(2):
            between = _dot(later_ref[...], hl_ref[h])
            for c in range(tk // SB_CHUNK):
                rows = slice(c * SB_CHUNK, (c + 1) * SB_CHUNK)
                a_ref[h, rows, :] = jnp.exp2(zl_ref[h, rows, :] + between[rows]).astype(BF16)
        i_b, j_b = unit(s, 1)
        keep_b = jnp.where(i_b == j_b, 0.0, 1.0)
        for h in range(2):
            before = carry_ref[h] * keep_b
            part = jnp.zeros((8, tq), F32)
            for c in range(tk // SB_CHUNK):
                rows = slice(c * SB_CHUNK, (c + 1) * SB_CHUNK)
                z = z_ref[h, rows, :]
                nz = -z
                log_stay = jnp.minimum(nz, 0.0) - jnp.log(1.0 + jnp.exp2(jnp.minimum(z, nz))) * LOG2_E
                hl_ref[h, rows, :] = log_stay.astype(BF16)
                zl_ref[h, rows, :] = z + log_stay + before
                part = part + jnp.sum(log_stay.reshape(SB_CHUNK // 8, 8, tq), axis=0)
            carry_ref[h] = before + jnp.sum(part, axis=0, keepdims=True)
        i_a, j_a = unit(s, 0)
        bias = bias_ref[jnp.where(i_a == j_a, 1, 0)]
        q_rows = pl.ds(pl.multiple_of(i_a * tq, tq), tq)
        k_rows = pl.ds(pl.multiple_of(j_a * tk, tk), tk)
        for h in range(2):
            z_ref[h] = lax.dot_general(k_ref[k_rows, :], qh_ref[h, q_rows, :], _NT,
                                       preferred_element_type=F32) + bias
        return carry

    lax.fori_loop(0, n_units + SB_STAGES - 1, step, 0, unroll=4)

    def finish(c, carry):
        o_ref[pl.ds(pl.multiple_of(c * tq, tq), tq), :] = fin_ref[c].T.astype(o_ref.dtype)
        return carry

    lax.fori_loop(0, seq // tq, finish, 0)


def _stick_breaking(qkv, batch, seq, heads, tq=256):
    n = qkv.shape[0]
    nt = seq // tq
    pairs = heads // 2
    units = [(i, j) for i in range(nt) for j in range(i, -1, -1)]
    ti = jnp.array([u[0] for u in units], jnp.int32)
    tj = jnp.array([u[1] for u in units], jnp.int32)
    kern = functools.partial(_sb_kernel, tq=tq, n_units=len(units))
    grid_spec = pltpu.PrefetchScalarGridSpec(
        num_scalar_prefetch=2,
        grid=(batch, pairs),
        in_specs=[
            pl.BlockSpec((seq, LANES), lambda b, h, ti, tj: (b, h)),
            pl.BlockSpec((seq, LANES), lambda b, h, ti, tj: (b, pairs + h)),
            pl.BlockSpec((seq, LANES), lambda b, h, ti, tj: (b, 2 * pairs + h)),
        ],
        out_specs=pl.BlockSpec((seq, LANES), lambda b, h, ti, tj: (b, h)),
        scratch_shapes=[
            pltpu.VMEM((2, seq, 2 * HEAD_DIM), BF16),
            pltpu.VMEM((nt, 2 * HEAD_DIM, tq), BF16),
            pltpu.VMEM((tq, tq), BF16),
            pltpu.VMEM((2, tq, tq), F32),
            pltpu.VMEM((2, tq, tq), F32),
            pltpu.VMEM((2, tq, tq), BF16),
            pltpu.VMEM((2, tq, tq), F32),
            pltpu.VMEM((2, tq, tq), BF16),
            pltpu.VMEM((2, 2 * HEAD_DIM, tq), F32),
            pltpu.VMEM((2, 1, tq), F32),
            pltpu.VMEM((nt, 2 * HEAD_DIM, tq), F32),
        ],
    )
    return pl.pallas_call(
        kern,
        grid_spec=grid_spec,
        out_shape=jax.ShapeDtypeStruct((n, heads * HEAD_DIM), BF16),
        compiler_params=_params("parallel", "parallel"),
        name="stick_breaking",
    )(ti, tj, qkv, qkv, qkv)


def _overlap_matrix(nch, nblk):
    c0 = jnp.arange(nch) * CMP_STRIDE
    s0 = jnp.arange(LANES) * SEL_BLOCK
    lo = jnp.maximum(c0[:, None], s0[None, :])
    hi = jnp.minimum(c0[:, None] + CMP_BLOCK, s0[None, :] + SEL_BLOCK)
    ovl = jnp.maximum(hi - lo, 0).astype(F32) / CMP_BLOCK
    keep = (jnp.arange(nch)[:, None] < nch - 1) & (jnp.arange(LANES)[None, :] < nblk)
    return jnp.where(keep, ovl, 0.0).astype(BF16)


def _conv_nsa_mixer(x, g, w_in, conv_w, pe_k, w1_k, w2_k, pe_v, w1_v, w2_v, w_out, batch, seq):
    d = HEAD_DIM
    cd = conv_w.shape[1]
    kvd = NSA_KV_HEADS * d
    o_q = 3 * cd
    o_kv = o_q + NSA_KV_HEADS * NSA_GROUP * d
    o_g = o_kv + 6 * kvd
    ng = NSA_GROUP * 3
    kv_cols = []
    gate_cols = []
    for h in range(NSA_KV_HEADS):
        for typ in range(3):
            base = o_kv + typ * 2 * kvd + h * d
            kv_cols += [w_in[:, base:base + d], w_in[:, base + kvd:base + kvd + d]]
        gate_cols += [w_in[:, o_g + h * ng:o_g + (h + 1) * ng], jnp.zeros((w_in.shape[0], LANES - ng), F32)]
    w_all = jnp.concatenate(
        [w_in[:, :o_q], w_in[:, o_q:o_kv] * (Q_SCALE * LOG2_E)] + kv_cols + gate_cols, axis=1).astype(BF16)
    y_conv, q, kv, gates = _proj_conv(
        x, g, w_all, conv_w,
        [(o_kv - o_q, BF16), (6 * kvd, BF16), (NSA_KV_HEADS * LANES, F32)], seq)

    nch = seq // CMP_STRIDE
    kv6 = kv.reshape(batch, seq, NSA_KV_HEADS, 3, 2, d)
    cm = kv6[:, :, :, 0].transpose(0, 2, 3, 1, 4).reshape(batch * NSA_KV_HEADS * 2 * nch, CMP_STRIDE * d)
    w1 = jnp.stack([w1_k, w1_v]).astype(BF16)
    w2 = jnp.stack([w2_k, w2_v]).astype(BF16)
    pe = jnp.stack([pe_k.reshape(1, -1), pe_v.reshape(1, -1)])
    kvc = _compress(cm, w1, w2, pe, nch)

    ovl = _overlap_matrix(nch, seq // SEL_BLOCK)
    o_cmp, sel = _cmp_branch(q, kvc, ovl, gates, batch, seq)
    o_sel = _nsa_attn(q, kv, sel, gates, batch, seq, "sel")
    o_win = _nsa_attn(q, kv, sel, gates, batch, seq, "win")
    return [y_conv, o_cmp, o_sel, o_win], w_out.astype(BF16)


def _stick_breaking_mixer(x, g, w_qkv, w_out, batch, seq):
    hd = w_out.shape[0]
    w = jnp.concatenate([w_qkv[:, :hd] * (Q_SCALE * LOG2_E), w_qkv[:, hd:]], axis=1).astype(BF16)
    (qkv,) = _norm_proj(x, g, w, [(3 * hd, BF16)])
    y = _stick_breaking(qkv, batch, seq, hd // HEAD_DIM)
    return [y], w_out.astype(BF16)


def kernel(x, norm_ffn1, w_ffn1_in, w_ffn1_out, norm_mix, w_in_ab, conv_w, cmp_pe_k, cmp_w1_k, cmp_w2_k,
           cmp_pe_v, cmp_w1_v, cmp_w2_v, w_out_ab, w_qkv_sb, w_out_sb, norm_ffn2, w_ffn2_in, w_ffn2_out,
           norm_final):
    batch, seq, d_model = x.shape
    depth = norm_ffn1.shape[0]
    x = x.reshape(batch * seq, d_model)
    for layer in range(depth):
        x = _ffn(x, norm_ffn1[layer], w_ffn1_in[layer].astype(BF16), w_ffn1_out[layer].astype(BF16))
        i = layer // 2
        if layer % 2 == 0:
            mix = _conv_nsa_mixer(x, norm_mix[layer], w_in_ab[i], conv_w[i], cmp_pe_k[i], cmp_w1_k[i],
                                  cmp_w2_k[i], cmp_pe_v[i], cmp_w1_v[i], cmp_w2_v[i], w_out_ab[i], batch, seq)
        else:
            mix = _stick_breaking_mixer(x, norm_mix[layer], w_qkv_sb[i], w_out_sb[i], batch, seq)
        x = _ffn(x, norm_ffn2[layer], w_ffn2_in[layer].astype(BF16), w_ffn2_out[layer].astype(BF16), mix=mix,
                 final_gain=norm_final if layer == depth - 1 else None)
    return x.reshape(batch, seq, d_model)
```

```python
import functools

import jax
import jax.numpy as jnp
from jax import lax
from jax.experimental import pallas as pl
from jax.experimental.pallas import tpu as pltpu

F32 = jnp.float32
BF16 = jnp.bfloat16

EPS = 1e-6
NEG = -1e30
HEAD_DIM = 64
Q_SCALE = HEAD_DIM ** -0.5
LOG2_E = 1.4426950408889634
NSA_KV_HEADS = 2
NSA_GROUP = 4
CMP_BLOCK = 32
CMP_STRIDE = 16
SEL_BLOCK = 64
SEL_SHIFT = SEL_BLOCK.bit_length() - 1
SEL_TOPK = 16
WINDOW = 512
FORCE_BONUS = 1e4
LANES = 128
MXU_TILE = 256
VMEM_LIMIT = 56 * 1024 * 1024

ROW_TILE = 512
ATTN_TILE = 256
CMP_TILE = 512

_NT = (((1,), (1,)), ((), ()))


def _params(*sem):
    return pltpu.CompilerParams(dimension_semantics=sem, vmem_limit_bytes=VMEM_LIMIT)


def _rms(x, g):
    ms = jnp.mean(x * x, axis=-1, keepdims=True)
    return x * lax.rsqrt(ms + EPS) * g


def _dot(a, b):
    return jnp.dot(a, b, preferred_element_type=F32)


def _ffn_kernel(x_ref, g_ref, wi_ref, wo_ref, *rest, chunks, n_mix, final):
    o_ref = rest[-1]
    f = wo_ref.shape[0]
    x = x_ref[...]
    if n_mix:
        ys, wm_ref = rest[:n_mix], rest[n_mix]
        rows = ys[0].shape[1]
        x = x + _dot(ys[0][...], wm_ref[0:rows, :])
        if n_mix > 1:
            y_sum = ys[1][...]
            for y_ref in ys[2:]:
                y_sum = y_sum + y_ref[...]
            x = x + _dot(y_sum.astype(BF16), wm_ref[rows:, :])
    h = _rms(x, g_ref[...]).astype(BF16)
    acc = None
    for lo, hi in chunks:
        gate = _dot(h, wi_ref[:, lo:hi])
        up = _dot(h, wi_ref[:, f + lo:f + hi])
        act = gate * jax.nn.sigmoid(gate) * up
        part = _dot(act.astype(BF16), wo_ref[lo:hi, :])
        acc = part if acc is None else acc + part
    out = x + 0.5 * acc
    if final:
        out = _rms(out, rest[-2][...])
    o_ref[...] = out


def _ffn(x, g, w_in, w_out, mix=None, final_gain=None, tm=ROW_TILE):
    n, d = x.shape
    f = w_out.shape[0]
    split = (f // 2 + MXU_TILE - 1) // MXU_TILE * MXU_TILE
    chunks = ((0, split), (split, f)) if 0 < split < f else ((0, f),)
    resident = pl.Buffered(1)
    in_specs = [
        pl.BlockSpec((tm, d), lambda i: (i, 0)),
        pl.BlockSpec((1, d), lambda i: (0, 0)),
        pl.BlockSpec(w_in.shape, lambda i: (0, 0), pipeline_mode=resident),
        pl.BlockSpec(w_out.shape, lambda i: (0, 0), pipeline_mode=resident),
    ]
    args = [x, g.reshape(1, d), w_in, w_out]
    n_mix = 0
    if mix is not None:
        ys, w_mix = mix
        n_mix = len(ys)
        in_specs += [pl.BlockSpec((tm, y.shape[1]), lambda i: (i, 0)) for y in ys]
        in_specs.append(pl.BlockSpec(w_mix.shape, lambda i: (0, 0), pipeline_mode=resident))
        args += [*ys, w_mix]
    if final_gain is not None:
        in_specs.append(pl.BlockSpec((1, d), lambda i: (0, 0)))
        args.append(final_gain.reshape(1, d))
    return pl.pallas_call(
        functools.partial(_ffn_kernel, chunks=chunks, n_mix=n_mix, final=final_gain is not None),
        grid=(n // tm,),
        in_specs=in_specs,
        out_specs=pl.BlockSpec((tm, d), lambda i: (i, 0)),
        out_shape=jax.ShapeDtypeStruct((n, d), F32),
        compiler_params=_params("parallel"),
        name="ffn",
    )(*args)


def _proj_kernel(x_ref, g_ref, w_ref, *o_refs):
    h = _rms(x_ref[...], g_ref[...]).astype(BF16)
    off = 0
    for o_ref in o_refs:
        width = o_ref.shape[1]
        o_ref[...] = _dot(h, w_ref[:, off:off + width]).astype(o_ref.dtype)
        off += width


def _norm_proj(x, g, w, outs, tm=ROW_TILE):
    n, d = x.shape
    return pl.pallas_call(
        _proj_kernel,
        grid=(n // tm,),
        in_specs=[
            pl.BlockSpec((tm, d), lambda i: (i, 0)),
            pl.BlockSpec((1, d), lambda i: (0, 0)),
            pl.BlockSpec(w.shape, lambda i: (0, 0)),
        ],
        out_specs=[pl.BlockSpec((tm, wd), lambda i: (i, 0)) for wd, _ in outs],
        out_shape=[jax.ShapeDtypeStruct((n, wd), dt) for wd, dt in outs],
        compiler_params=_params("parallel"),
        name="norm_proj",
    )(x, g.reshape(1, d), w)


HALO = 8


def _proj_conv_kernel(x_ref, prev_ref, g_ref, w_ref, cw_ref, yc_ref, *rest, tiles_per_seq, cd):
    o_refs, ext_ref = rest[:-1], rest[-1]
    tm = x_ref.shape[0]
    first = (pl.program_id(0) % tiles_per_seq) == 0
    x_ext = jnp.concatenate([prev_ref[...], x_ref[...]], axis=0)
    h = _rms(x_ext, g_ref[...]).astype(BF16)
    cv = _dot(h, w_ref[:, 0:3 * cd])
    u = cv[:, cd:2 * cd] * cv[:, 2 * cd:3 * cd]
    ext_ref[...] = u
    ext_ref[0:HALO, :] = jnp.where(first, 0.0, u[0:HALO])
    w = cw_ref[...]
    y = (w[0:1] * ext_ref[HALO - 2:HALO - 2 + tm, :] + w[1:2] * ext_ref[HALO - 1:HALO - 1 + tm, :]
         + w[2:3] * u[HALO:])
    yc_ref[...] = (cv[HALO:, 0:cd] * y).astype(yc_ref.dtype)
    off = 3 * cd
    for o_ref in o_refs:
        width = o_ref.shape[1]
        o_ref[...] = _dot(h, w_ref[:, off:off + width])[HALO:].astype(o_ref.dtype)
        off += width


def _proj_conv(x, g, w, conv_w, outs, seq, tm=ROW_TILE):
    n, d = x.shape
    cd = conv_w.shape[1]
    kern = functools.partial(_proj_conv_kernel, tiles_per_seq=seq // tm, cd=cd)
    return pl.pallas_call(
        kern,
        grid=(n // tm,),
        in_specs=[
            pl.BlockSpec((tm, d), lambda i: (i, 0)),
            pl.BlockSpec((HALO, d), lambda i: (jnp.maximum(i * (tm // HALO) - 1, 0), 0)),
            pl.BlockSpec((1, d), lambda i: (0, 0)),
            pl.BlockSpec(w.shape, lambda i: (0, 0)),
            pl.BlockSpec(conv_w.shape, lambda i: (0, 0)),
        ],
        out_specs=[pl.BlockSpec((tm, cd), lambda i: (i, 0))]
        + [pl.BlockSpec((tm, wd), lambda i: (i, 0)) for wd, _ in outs],
        out_shape=[jax.ShapeDtypeStruct((n, cd), BF16)]
        + [jax.ShapeDtypeStruct((n, wd), dt) for wd, dt in outs],
        scratch_shapes=[pltpu.VMEM((tm + HALO, cd), F32)],
        compiler_params=_params("parallel"),
        name="proj_conv",
    )(x, x, g.reshape(1, d), w, conv_w)


def _compress_kernel(kvr_ref, w1c_ref, w1_ref, w2_ref, pe_ref, o_ref, *, tok_stride, group_stride):
    nch = kvr_ref.shape[0]
    width = w1c_ref.shape[2]
    hidden = w1_ref.shape[2]
    biases = []
    for t in range(2):
        pe = jnp.broadcast_to(pe_ref[t], (8, pe_ref.shape[2])).astype(BF16)
        biases.append(_dot(pe, w1_ref[t])[0:1])
    bias = jnp.concatenate(biases, axis=1)
    for gi in range(NSA_KV_HEADS):
        a = jnp.zeros((nch, 2 * hidden), F32)
        b = jnp.zeros((nch, 2 * hidden), F32)
        for l in range(CMP_STRIDE):
            off = l * tok_stride + gi * group_stride
            tok = kvr_ref[:, off:off + width]
            a = a + _dot(tok, w1c_ref[0, l])
            b = b + _dot(tok, w1c_ref[1, l])
        act = jax.nn.gelu(a + pltpu.roll(b, nch - 1, 0) + bias).astype(BF16)
        out = [_dot(act[:, t * hidden:(t + 1) * hidden], w2_ref[t]) for t in range(2)]
        o_ref[gi * nch:(gi + 1) * nch, :] = jnp.concatenate(out, axis=1).astype(o_ref.dtype)


def _compress(kvr, w1c, w1, w2, pe, nch, tok_stride, group_stride):
    batch = kvr.shape[0] // nch
    d = w2.shape[2]
    kern = functools.partial(_compress_kernel, tok_stride=tok_stride, group_stride=group_stride)
    return pl.pallas_call(
        kern,
        grid=(batch,),
        in_specs=[
            pl.BlockSpec((nch, kvr.shape[1]), lambda i: (i, 0)),
            pl.BlockSpec(w1c.shape, lambda i: (0, 0, 0, 0)),
            pl.BlockSpec(w1.shape, lambda i: (0, 0, 0)),
            pl.BlockSpec(w2.shape, lambda i: (0, 0, 0)),
            pl.BlockSpec(pe.shape, lambda i: (0, 0, 0)),
        ],
        out_specs=pl.BlockSpec((NSA_KV_HEADS * nch, 2 * d), lambda i: (i, 0)),
        out_shape=jax.ShapeDtypeStruct((batch * NSA_KV_HEADS * nch, 2 * d), BF16),
        compiler_params=_params("parallel"),
        name="compress",
    )(kvr, w1c, w1, w2, pe)


def _cmp_kernel(q_ref, kvc_ref, ovl_ref, gt_ref, o_ref, sel_ref, *, nblk):
    tq = q_ref.shape[0]
    nc = kvc_ref.shape[0]
    d = HEAD_DIM
    i = pl.program_id(2)
    q = q_ref[...]
    kvc = kvc_ref[...]
    kc = kvc[:, 0:d]
    pos = i * tq + lax.broadcasted_iota(jnp.int32, (tq, nc), 0)
    cidx = lax.broadcasted_iota(jnp.int32, (tq, nc), 1)
    cmask = cidx * CMP_STRIDE + (CMP_BLOCK - 1) <= pos
    gate = jax.nn.sigmoid(gt_ref[...])
    imp = jnp.zeros((tq, LANES), F32)
    outs = []
    for r in range(NSA_GROUP):
        s = lax.dot_general(q[:, r * d:(r + 1) * d], kc, _NT, preferred_element_type=F32)
        s = jnp.where(cmask, s, NEG)
        e = jnp.exp2(s - jnp.max(s, axis=1, keepdims=True))
        p = jnp.where(cmask, e / jnp.sum(e, axis=1, keepdims=True), 0.0)
        pb = p.astype(BF16)
        o = _dot(pb, kvc)[:, d:2 * d]
        outs.append(o * gate[:, 3 * r:3 * r + 1])
        imp = imp + _dot(pb, ovl_ref[...])
    o_ref[...] = jnp.concatenate(outs, axis=1)

    posb = i * tq + lax.broadcasted_iota(jnp.int32, (tq, LANES), 0)
    blk = lax.broadcasted_iota(jnp.int32, (tq, LANES), 1)
    cur = posb >> SEL_SHIFT
    forced = jnp.where(blk == 0, 1.0, jnp.where(blk == cur, 1.0, jnp.where(blk == cur - 1, 1.0, 0.0)))
    score = jnp.where(blk * SEL_BLOCK <= posb, imp + forced * FORCE_BONUS, NEG)
    sc = score.T[0:nblk, :]
    grp = 8
    sub = lax.broadcasted_iota(jnp.int32, (grp, tq), 0)
    groups = [sc[g0:g0 + grp, :] for g0 in range(0, nblk, grp)]
    ranks = [jnp.zeros((grp, tq), F32) for _ in groups]
    for sp in range(nblk):
        other = sc[sp:sp + 1, :]
        for gi, mine in enumerate(groups):
            g0 = gi * grp
            ge = jnp.where(other >= mine, 1.0, 0.0)
            gt = jnp.where(other > mine, 1.0, 0.0)
            if g0 > sp:
                beats = ge
            elif g0 + grp - 1 < sp:
                beats = gt
            else:
                beats = jnp.where(sub + g0 > sp, ge, gt)
            ranks[gi] = ranks[gi] + beats
    chosen = jnp.where(jnp.concatenate(ranks, axis=0) < SEL_TOPK, 1.0, 0.0)
    chosen = jnp.concatenate([chosen, jnp.zeros((LANES - nblk, tq), F32)], axis=0).astype(sel_ref.dtype)
    tile = sel_ref.shape[2]
    for t in range(tq // tile):
        sel_ref[t] = chosen[:, t * tile:(t + 1) * tile]


def _cmp_branch(q, kvc, ovl, gates, batch, seq, tq=CMP_TILE, sel_tile=ATTN_TILE):
    n = q.shape[0]
    nt = seq // tq
    g = NSA_KV_HEADS
    wq = NSA_GROUP * HEAD_DIM
    nc = kvc.shape[0] // (batch * g)
    kern = functools.partial(_cmp_kernel, nblk=seq // SEL_BLOCK)
    return pl.pallas_call(
        kern,
        grid=(batch, g, nt),
        in_specs=[
            pl.BlockSpec((tq, wq), lambda b, h, i: (b * nt + i, h)),
            pl.BlockSpec((nc, 2 * HEAD_DIM), lambda b, h, i: (b * g + h, 0)),
            pl.BlockSpec(ovl.shape, lambda b, h, i: (0, 0)),
            pl.BlockSpec((tq, LANES), lambda b, h, i: (b * nt + i, h)),
        ],
        out_specs=[
            pl.BlockSpec((tq, wq), lambda b, h, i: (b * nt + i, h)),
            pl.BlockSpec((tq // sel_tile, LANES, sel_tile), lambda b, h, i: ((b * g + h) * nt + i, 0, 0)),
        ],
        out_shape=[
            jax.ShapeDtypeStruct((n, g * wq), F32),
            jax.ShapeDtypeStruct((batch * g * seq // sel_tile, LANES, sel_tile), BF16),
        ],
        compiler_params=_params("parallel", "parallel", "parallel"),
        name="nsa_cmp_select",
    )(q, kvc, ovl, gates)


NSA_STAGES = 3
NSA_CHUNK = 32
MASK_BIG = 1e30


def _nsa_attn_kernel(ti_ref, tj_ref, tm_ref, tf_ref, q_ref, kv_ref, *rest, mode, branch, tq, n_units):
    if mode == "sel":
        (selt_ref, gt_ref, o_ref, qh_ref, kk_ref, vat_ref, ext_ref, bias_ref, s_ref, p_ref, al_ref, m_ref,
         acc_ref, fin_ref) = rest
    else:
        gt_ref, o_ref, qh_ref, kk_ref, vat_ref, bias_ref, s_ref, p_ref, al_ref, m_ref, acc_ref, fin_ref = rest
    tk = tq
    d = HEAD_DIM
    seq = q_ref.shape[0]
    key = lax.broadcasted_iota(jnp.int32, (tk, tq), 0)
    qry = lax.broadcasted_iota(jnp.int32, (tk, tq), 1)
    bias_ref[0] = jnp.zeros((tk, tq), F32)
    bias_ref[1] = jnp.where(key <= qry, 0.0, NEG)
    bias_ref[2] = jnp.where(key > qry, 0.0, NEG)
    s_ref[...] = jnp.zeros_like(s_ref)
    p_ref[...] = jnp.zeros_like(p_ref)
    al_ref[...] = jnp.zeros_like(al_ref)
    acc_ref[...] = jnp.zeros_like(acc_ref)
    m_ref[...] = jnp.full(m_ref.shape, NEG, F32)
    lane = lax.broadcasted_iota(jnp.int32, (tq, 2 * d), 1)
    erow = lax.broadcasted_iota(jnp.int32, (tk, LANES), 0) >> SEL_SHIFT
    ecol = lax.broadcasted_iota(jnp.int32, (tk, LANES), 1)

    def prep(c, carry):
        rows = pl.ds(pl.multiple_of(c * tq, tq), tq)
        q = q_ref[rows, :]
        for r in range(NSA_GROUP):
            pair = q[:, (r // 2) * 2 * d:(r // 2 + 1) * 2 * d]
            mine = (lane < d) if r % 2 == 0 else (lane >= d)
            qh_ref[r, rows, :] = jnp.where(mine, pair, 0.0).astype(BF16)
        kv = kv_ref[rows, :]
        k_first = kv[:, 0:d]
        kk_ref[rows, :] = jnp.concatenate([k_first, k_first], axis=1)
        v_first = jnp.concatenate([kv[:, d:2 * d], k_first], axis=1)
        vat_ref[c] = jnp.where(lane < d, v_first, 1.0).astype(F32).T.astype(BF16)
        if mode == "sel":
            ext_ref[c] = jnp.where(ecol == c * (tk // SEL_BLOCK) + erow, 1.0, 0.0).astype(BF16)
        return carry

    lax.fori_loop(0, seq // tq, prep, 0)

    def unit(s, delay):
        u = jnp.clip(s - delay, 0, n_units - 1)
        return ti_ref[u], tj_ref[u], tm_ref[u], tf_ref[u]

    def step(s, carry):
        i_c, j_c, _, f_c = unit(s, 2)
        vat = vat_ref[j_c]
        keep = jnp.where((f_c & 1) == 1, 0.0, 1.0)
        for r in range(NSA_GROUP):
            acc = acc_ref[r] * (al_ref[r] * keep) + _dot(vat, p_ref[r])
            acc_ref[r] = acc
            fin_ref[i_c, r] = acc
        _, _, _, f_b = unit(s, 1)
        first_b = (f_b & 1) == 1
        for r in range(NSA_GROUP):
            m_prev = jnp.where(first_b, NEG, m_ref[r])
            top = None
            for c in range(tk // NSA_CHUNK):
                rows = slice(c * NSA_CHUNK, (c + 1) * NSA_CHUNK)
                part = jnp.max(s_ref[r, rows, :].reshape(NSA_CHUNK // 8, 8, tq), axis=0)
                top = part if top is None else jnp.maximum(top, part)
            m_new = jnp.maximum(m_prev, jnp.max(top, axis=0, keepdims=True))
            al_ref[r] = jnp.exp2(m_prev - m_new)
            m_ref[r] = m_new
            for c in range(tk // NSA_CHUNK):
                rows = slice(c * NSA_CHUNK, (c + 1) * NSA_CHUNK)
                p_ref[r, rows, :] = jnp.exp2(s_ref[r, rows, :] - m_new).astype(BF16)
        i_a, j_a, t_a, _ = unit(s, 0)
        q_rows = pl.ds(pl.multiple_of(i_a * tq, tq), tq)
        if mode == "sel":
            chosen = _dot(ext_ref[j_a], selt_ref[i_a])
            bias_ref[3] = bias_ref[t_a] + (chosen - 1.0) * MASK_BIG
            t_a = 3
        k = kk_ref[pl.ds(pl.multiple_of(j_a * tk, tk), tk), :]
        for r in range(NSA_GROUP):
            s_ref[r] = (lax.dot_general(k, qh_ref[r, q_rows, :], _NT, preferred_element_type=F32)
                        + bias_ref[t_a])
        return carry

    lax.fori_loop(0, n_units + NSA_STAGES - 1, step, 0, unroll=3 if mode == "win" else 4)

    def finish(c, carry):
        o_rows = pl.ds(pl.multiple_of(c * tq, tq), tq)
        gate = jax.nn.sigmoid(gt_ref[o_rows, :]).T
        outs = []
        for r in range(NSA_GROUP):
            acc = fin_ref[c, r]
            outs.append(acc[0:d] / acc[d:d + 1] * gate[3 * r + branch:3 * r + branch + 1])
        o_ref[o_rows, :] = jnp.concatenate(outs, axis=0).T
        return carry

    lax.fori_loop(0, seq // tq, finish, 0)


def _nsa_attn(q, kv, selt, gates, batch, seq, mode, tq=ATTN_TILE):
    n = q.shape[0]
    nt = seq // tq
    g = NSA_KV_HEADS
    wq = NSA_GROUP * HEAD_DIM
    branch = 1 if mode == "sel" else 2
    assert WINDOW == 2 * tq
    if mode == "sel":
        units = [(i, j, int(j == i), int(j == 0) | 2 * int(j == i)) for i in range(nt) for j in range(i + 1)]
    else:
        units = [(i, j, (1, 0, 2)[i - j], int(j == max(i - 2, 0)) | 2 * int(j == i))
                 for i in range(nt) for j in range(max(i - 2, 0), i + 1)]
    tables = [jnp.array([u[c] for u in units], jnp.int32) for c in range(4)]
    kern = functools.partial(_nsa_attn_kernel, mode=mode, branch=branch, tq=tq, n_units=len(units))
    in_specs = [
        pl.BlockSpec((seq, wq), lambda b, h, *_: (b, h)),
        pl.BlockSpec((seq, 2 * HEAD_DIM), lambda b, h, *_: (b, h * 2 + branch - 1)),
    ]
    args = [q, kv]
    scratch = [
        pltpu.VMEM((NSA_GROUP, seq, 2 * HEAD_DIM), BF16),
        pltpu.VMEM((seq, 2 * HEAD_DIM), BF16),
        pltpu.VMEM((nt, 2 * HEAD_DIM, tq), BF16),
    ]
    if mode == "sel":
        in_specs.append(pl.BlockSpec((nt, LANES, tq), lambda b, h, *_: (b * g + h, 0, 0)))
        args.append(selt)
        scratch.append(pltpu.VMEM((nt, tq, LANES), BF16))
    in_specs.append(pl.BlockSpec((seq, LANES), lambda b, h, *_: (b, h)))
    args.append(gates)
    scratch += [
        pltpu.VMEM((4, tq, tq), F32),
        pltpu.VMEM((NSA_GROUP, tq, tq), F32),
        pltpu.VMEM((NSA_GROUP, tq, tq), BF16),
        pltpu.VMEM((NSA_GROUP, 1, tq), F32),
        pltpu.VMEM((NSA_GROUP, 1, tq), F32),
        pltpu.VMEM((NSA_GROUP, 2 * HEAD_DIM, tq), F32),
        pltpu.VMEM((nt, NSA_GROUP, 2 * HEAD_DIM, tq), F32),
    ]
    grid_spec = pltpu.PrefetchScalarGridSpec(
        num_scalar_prefetch=4,
        grid=(batch, g),
        in_specs=in_specs,
        out_specs=pl.BlockSpec((seq, wq), lambda b, h, *_: (b, h)),
        scratch_shapes=scratch,
    )
    return pl.pallas_call(
        kern,
        grid_spec=grid_spec,
        out_shape=jax.ShapeDtypeStruct((n, g * wq), F32),
        compiler_params=_params("parallel", "parallel"),
        name="nsa_" + mode,
    )(*tables, *args)


SB_STAGES = 4
SB_CHUNK = 32


def _sb_kernel(ti_ref, tj_ref, q_ref, k_ref, v_ref, o_ref, qh_ref, vt_ref, later_ref, bias_ref,
               z_ref, hl_ref, zl_ref, a_ref, acc_ref, carry_ref, fin_ref, *, tq, n_units):
    tk = tq
    d = HEAD_DIM
    seq = q_ref.shape[0]
    key = lax.broadcasted_iota(jnp.int32, (tk, tq), 0)
    qry = lax.broadcasted_iota(jnp.int32, (tk, tq), 1)
    later_ref[...] = jnp.where(qry > key, 1.0, 0.0).astype(BF16)
    bias_ref[0] = jnp.zeros((tk, tq), F32)
    bias_ref[1] = jnp.where(key < qry, 0.0, NEG)
    z_ref[...] = jnp.zeros_like(z_ref)
    hl_ref[...] = jnp.zeros_like(hl_ref)
    zl_ref[...] = jnp.zeros_like(zl_ref)
    a_ref[...] = jnp.zeros_like(a_ref)
    acc_ref[...] = jnp.zeros_like(acc_ref)
    carry_ref[...] = jnp.zeros_like(carry_ref)

    lane = lax.broadcasted_iota(jnp.int32, (tq, 2 * d), 1)

    def split(c, carry):
        rows = pl.ds(pl.multiple_of(c * tq, tq), tq)
        q = q_ref[rows, :]
        qh_ref[0, rows, :] = jnp.where(lane < d, q, 0.0).astype(BF16)
        qh_ref[1, rows, :] = jnp.where(lane >= d, q, 0.0).astype(BF16)
        vt_ref[c] = v_ref[rows, :].astype(F32).T.astype(BF16)
        return carry

    lax.fori_loop(0, seq // tq, split, 0)
    feat = lax.broadcasted_iota(jnp.int32, (2 * d, tq), 0)

    def unit(s, delay):
        u = jnp.clip(s - delay, 0, n_units - 1)
        return ti_ref[u], tj_ref[u]

    def step(s, carry):
        i_e, j_e = unit(s, 3)
        vt = vt_ref[j_e]
        keep = jnp.where(i_e == j_e, 0.0, 1.0)
        accs = []
        for h in range(2):
            acc = acc_ref[h] * keep + _dot(vt, a_ref[h])
            acc_ref[h] = acc
            accs.append(acc)
        fin_ref[i_e] = jnp.where(feat < d, accs[0], accs[1])
        for h in range(2):
            between = _dot(later_ref[...], hl_ref[h])
            for c in range(tk // SB_CHUNK):
                rows = slice(c * SB_CHUNK, (c + 1) * SB_CHUNK)
                a_ref[h, rows, :] = jnp.exp2(zl_ref[h, rows, :] + between[rows]).astype(BF16)
        i_b, j_b = unit(s, 1)
        keep_b = jnp.where(i_b == j_b, 0.0, 1.0)
        for h in range(2):
            before = carry_ref[h] * keep_b
            part = jnp.zeros((8, tq), F32)
            for c in range(tk // SB_CHUNK):
                rows = slice(c * SB_CHUNK, (c + 1) * SB_CHUNK)
                z = z_ref[h, rows, :]
                nz = -z
                log_stay = jnp.minimum(nz, 0.0) - jnp.log(1.0 + jnp.exp2(jnp.minimum(z, nz))) * LOG2_E
                hl_ref[h, rows, :] = log_stay.astype(BF16)
                zl_ref[h, rows, :] = z + log_stay + before
                part = part + jnp.sum(log_stay.reshape(SB_CHUNK // 8, 8, tq), axis=0)
            carry_ref[h] = before + jnp.sum(part, axis=0, keepdims=True)
        i_a, j_a = unit(s, 0)
        bias = bias_ref[jnp.where(i_a == j_a, 1, 0)]
        q_rows = pl.ds(pl.multiple_of(i_a * tq, tq), tq)
        k_rows = pl.ds(pl.multiple_of(j_a * tk, tk), tk)
        for h in range(2):
            z_ref[h] = lax.dot_general(k_ref[k_rows, :], qh_ref[h, q_rows, :], _NT,
                                       preferred_element_type=F32) + bias
        return carry

    lax.fori_loop(0, n_units + SB_STAGES - 1, step, 0, unroll=4)

    def finish(c, carry):
        o_ref[pl.ds(pl.multiple_of(c * tq, tq), tq), :] = fin_ref[c].T.astype(o_ref.dtype)
        return carry

    lax.fori_loop(0, seq // tq, finish, 0)


def _stick_breaking(qkv, batch, seq, heads, tq=ATTN_TILE):
    n = qkv.shape[0]
    nt = seq // tq
    pairs = heads // 2
    units = [(i, j) for i in range(nt) for j in range(i, -1, -1)]
    ti = jnp.array([u[0] for u in units], jnp.int32)
    tj = jnp.array([u[1] for u in units], jnp.int32)
    kern = functools.partial(_sb_kernel, tq=tq, n_units=len(units))
    grid_spec = pltpu.PrefetchScalarGridSpec(
        num_scalar_prefetch=2,
        grid=(batch, pairs),
        in_specs=[
            pl.BlockSpec((seq, LANES), lambda b, h, ti, tj: (b, h)),
            pl.BlockSpec((seq, LANES), lambda b, h, ti, tj: (b, pairs + h)),
            pl.BlockSpec((seq, LANES), lambda b, h, ti, tj: (b, 2 * pairs + h)),
        ],
        out_specs=pl.BlockSpec((seq, LANES), lambda b, h, ti, tj: (b, h)),
        scratch_shapes=[
            pltpu.VMEM((2, seq, 2 * HEAD_DIM), BF16),
            pltpu.VMEM((nt, 2 * HEAD_DIM, tq), BF16),
            pltpu.VMEM((tq, tq), BF16),
            pltpu.VMEM((2, tq, tq), F32),
            pltpu.VMEM((2, tq, tq), F32),
            pltpu.VMEM((2, tq, tq), BF16),
            pltpu.VMEM((2, tq, tq), F32),
            pltpu.VMEM((2, tq, tq), BF16),
            pltpu.VMEM((2, 2 * HEAD_DIM, tq), F32),
            pltpu.VMEM((2, 1, tq), F32),
            pltpu.VMEM((nt, 2 * HEAD_DIM, tq), F32),
        ],
    )
    return pl.pallas_call(
        kern,
        grid_spec=grid_spec,
        out_shape=jax.ShapeDtypeStruct((n, heads * HEAD_DIM), BF16),
        compiler_params=_params("parallel", "parallel"),
        name="stick_breaking",
    )(ti, tj, qkv, qkv, qkv)


def _overlap_matrix(nch, nblk):
    c0 = jnp.arange(nch) * CMP_STRIDE
    s0 = jnp.arange(LANES) * SEL_BLOCK
    lo = jnp.maximum(c0[:, None], s0[None, :])
    hi = jnp.minimum(c0[:, None] + CMP_BLOCK, s0[None, :] + SEL_BLOCK)
    ovl = jnp.maximum(hi - lo, 0).astype(F32) / CMP_BLOCK
    keep = (jnp.arange(nch)[:, None] < nch - 1) & (jnp.arange(LANES)[None, :] < nblk)
    return jnp.where(keep, ovl, 0.0).astype(BF16)


def _conv_nsa_mixer(x, g, w_in, conv_w, pe_k, w1_k, w2_k, pe_v, w1_v, w2_v, w_out, batch, seq):
    d = HEAD_DIM
    cd = conv_w.shape[1]
    kvd = NSA_KV_HEADS * d
    o_q = 3 * cd
    o_kv = o_q + NSA_KV_HEADS * NSA_GROUP * d
    o_g = o_kv + 6 * kvd
    ng = NSA_GROUP * 3
    cmp_cols, kv_cols, gate_cols = [], [], []
    for h in range(NSA_KV_HEADS):
        for typ in range(3):
            base = o_kv + typ * 2 * kvd + h * d
            pair = [w_in[:, base:base + d], w_in[:, base + kvd:base + kvd + d]]
            if typ == 0:
                cmp_cols += pair
            else:
                kv_cols += pair
        gate_cols += [w_in[:, o_g + h * ng:o_g + (h + 1) * ng], jnp.zeros((w_in.shape[0], LANES - ng), F32)]
    w_all = jnp.concatenate(
        [w_in[:, :o_q], w_in[:, o_q:o_kv] * (Q_SCALE * LOG2_E)] + cmp_cols + kv_cols + gate_cols,
        axis=1).astype(BF16)
    y_conv, q, kv_cmp, kv, gates = _proj_conv(
        x, g, w_all, conv_w,
        [(o_kv - o_q, BF16), (2 * kvd, BF16), (4 * kvd, BF16), (NSA_KV_HEADS * LANES, F32)], seq)

    nch = seq // CMP_STRIDE
    hid = w1_k.shape[1]
    wk = w1_k.reshape(2, CMP_STRIDE, d, hid)
    wv = w1_v.reshape(2, CMP_STRIDE, d, hid)
    zero = jnp.zeros_like(wk)
    w1c = jnp.concatenate([jnp.concatenate([wk, zero], axis=3), jnp.concatenate([zero, wv], axis=3)],
                          axis=2).astype(BF16)
    w1 = jnp.stack([w1_k, w1_v]).astype(BF16)
    w2 = jnp.stack([w2_k, w2_v]).astype(BF16)
    pe = jnp.stack([pe_k.reshape(1, -1), pe_v.reshape(1, -1)])
    kvr = kv_cmp.reshape(batch * nch, CMP_STRIDE * kv_cmp.shape[1])
    kvc = _compress(kvr, w1c, w1, w2, pe, nch, kv_cmp.shape[1], 2 * d)

    ovl = _overlap_matrix(nch, seq // SEL_BLOCK)
    o_cmp, sel = _cmp_branch(q, kvc, ovl, gates, batch, seq)
    o_sel = _nsa_attn(q, kv, sel, gates, batch, seq, "sel")
    o_win = _nsa_attn(q, kv, sel, gates, batch, seq, "win")
    return [y_conv, o_cmp, o_sel, o_win], w_out.astype(BF16)


def _stick_breaking_mixer(x, g, w_qkv, w_out, batch, seq):
    hd = w_out.shape[0]
    w = jnp.concatenate([w_qkv[:, :hd] * (Q_SCALE * LOG2_E), w_qkv[:, hd:]], axis=1).astype(BF16)
    (qkv,) = _norm_proj(x, g, w, [(3 * hd, BF16)])
    y = _stick_breaking(qkv, batch, seq, hd // HEAD_DIM)
    return [y], w_out.astype(BF16)


def kernel(x, norm_ffn1, w_ffn1_in, w_ffn1_out, norm_mix, w_in_ab, conv_w, cmp_pe_k, cmp_w1_k, cmp_w2_k,
           cmp_pe_v, cmp_w1_v, cmp_w2_v, w_out_ab, w_qkv_sb, w_out_sb, norm_ffn2, w_ffn2_in, w_ffn2_out,
           norm_final):
    batch, seq, d_model = x.shape
    depth = norm_ffn1.shape[0]
    assert seq % ROW_TILE == 0 and seq % CMP_TILE == 0 and seq % ATTN_TILE == 0, seq
    assert seq // SEL_BLOCK <= LANES and SEL_BLOCK == 1 << SEL_SHIFT
    x = x.reshape(batch * seq, d_model)
    for layer in range(depth):
        x = _ffn(x, norm_ffn1[layer], w_ffn1_in[layer].astype(BF16), w_ffn1_out[layer].astype(BF16))
        i = layer // 2
        if layer % 2 == 0:
            mix = _conv_nsa_mixer(x, norm_mix[layer], w_in_ab[i], conv_w[i], cmp_pe_k[i], cmp_w1_k[i],
                                  cmp_w2_k[i], cmp_pe_v[i], cmp_w1_v[i], cmp_w2_v[i], w_out_ab[i], batch, seq)
        else:
            mix = _stick_breaking_mixer(x, norm_mix[layer], w_qkv_sb[i], w_out_sb[i], batch, seq)
        x = _ffn(x, norm_ffn2[layer], w_ffn2_in[layer].astype(BF16), w_ffn2_out[layer].astype(BF16), mix=mix,
                 final_gain=norm_final if layer == depth - 1 else None)
    return x.reshape(batch, seq, d_model)
```

```python
import functools

import jax
import jax.numpy as jnp
from jax import lax
from jax.experimental import pallas as pl
from jax.experimental.pallas import tpu as pltpu

F32 = jnp.float32
BF16 = jnp.bfloat16

EPS = 1e-6
NEG = -1e30
HEAD_DIM = 64
Q_SCALE = HEAD_DIM ** -0.5
LOG2_E = 1.4426950408889634
NSA_KV_HEADS = 2
NSA_GROUP = 4
CMP_BLOCK = 32
CMP_STRIDE = 16
SEL_BLOCK = 64
SEL_SHIFT = SEL_BLOCK.bit_length() - 1
SEL_TOPK = 16
WINDOW = 512
FORCE_BONUS = 1e4
LANES = 128
MXU_TILE = 256
VMEM_LIMIT = 56 * 1024 * 1024

ROW_TILE = 512
ATTN_TILE = 256
CMP_TILE = 512

_NT = (((1,), (1,)), ((), ()))


def _params(*sem):
    return pltpu.CompilerParams(dimension_semantics=sem, vmem_limit_bytes=VMEM_LIMIT)


def _rms(x, g):
    ms = jnp.mean(x * x, axis=-1, keepdims=True)
    return x * lax.rsqrt(ms + EPS) * g


def _dot(a, b):
    return jnp.dot(a, b, preferred_element_type=F32)


def _ffn_kernel(x_ref, g_ref, wi_ref, wo_ref, *rest, chunks, n_mix, final):
    o_ref = rest[-1]
    f = wo_ref.shape[0]
    x = x_ref[...]
    if n_mix:
        ys, wm_ref = rest[:n_mix], rest[n_mix]
        rows = ys[0].shape[1]
        x = x + _dot(ys[0][...], wm_ref[0:rows, :])
        if n_mix > 1:
            y_sum = ys[1][...]
            for y_ref in ys[2:]:
                y_sum = y_sum + y_ref[...]
            x = x + _dot(y_sum.astype(BF16), wm_ref[rows:, :])
    h = _rms(x, g_ref[...]).astype(BF16)
    acc = None
    for lo, hi in chunks:
        gate = _dot(h, wi_ref[:, lo:hi])
        up = _dot(h, wi_ref[:, f + lo:f + hi])
        act = gate * jax.nn.sigmoid(gate) * up
        part = _dot(act.astype(BF16), wo_ref[lo:hi, :])
        acc = part if acc is None else acc + part
    out = x + 0.5 * acc
    if final:
        out = _rms(out, rest[-2][...])
    o_ref[...] = out


def _ffn(x, g, w_in, w_out, mix=None, final_gain=None, tm=ROW_TILE):
    n, d = x.shape
    f = w_out.shape[0]
    split = (f // 2 + MXU_TILE - 1) // MXU_TILE * MXU_TILE
    chunks = ((0, split), (split, f)) if 0 < split < f else ((0, f),)
    resident = pl.Buffered(1)
    in_specs = [
        pl.BlockSpec((tm, d), lambda i: (i, 0)),
        pl.BlockSpec((1, d), lambda i: (0, 0)),
        pl.BlockSpec(w_in.shape, lambda i: (0, 0), pipeline_mode=resident),
        pl.BlockSpec(w_out.shape, lambda i: (0, 0), pipeline_mode=resident),
    ]
    args = [x, g.reshape(1, d), w_in, w_out]
    n_mix = 0
    if mix is not None:
        ys, w_mix = mix
        n_mix = len(ys)
        in_specs += [pl.BlockSpec((tm, y.shape[1]), lambda i: (i, 0)) for y in ys]
        in_specs.append(pl.BlockSpec(w_mix.shape, lambda i: (0, 0), pipeline_mode=resident))
        args += [*ys, w_mix]
    if final_gain is not None:
        in_specs.append(pl.BlockSpec((1, d), lambda i: (0, 0)))
        args.append(final_gain.reshape(1, d))
    return pl.pallas_call(
        functools.partial(_ffn_kernel, chunks=chunks, n_mix=n_mix, final=final_gain is not None),
        grid=(n // tm,),
        in_specs=in_specs,
        out_specs=pl.BlockSpec((tm, d), lambda i: (i, 0)),
        out_shape=jax.ShapeDtypeStruct((n, d), F32),
        compiler_params=_params("parallel"),
        name="ffn",
    )(*args)


def _proj_kernel(x_ref, g_ref, w_ref, *o_refs):
    h = _rms(x_ref[...], g_ref[...]).astype(BF16)
    off = 0
    for o_ref in o_refs:
        width = o_ref.shape[1]
        o_ref[...] = _dot(h, w_ref[:, off:off + width]).astype(o_ref.dtype)
        off += width


def _norm_proj(x, g, w, outs, tm=ROW_TILE):
    n, d = x.shape
    return pl.pallas_call(
        _proj_kernel,
        grid=(n // tm,),
        in_specs=[
            pl.BlockSpec((tm, d), lambda i: (i, 0)),
            pl.BlockSpec((1, d), lambda i: (0, 0)),
            pl.BlockSpec(w.shape, lambda i: (0, 0)),
        ],
        out_specs=[pl.BlockSpec((tm, wd), lambda i: (i, 0)) for wd, _ in outs],
        out_shape=[jax.ShapeDtypeStruct((n, wd), dt) for wd, dt in outs],
        compiler_params=_params("parallel"),
        name="norm_proj",
    )(x, g.reshape(1, d), w)


HALO = 8


def _proj_conv_kernel(x_ref, prev_ref, g_ref, w_ref, cw_ref, yc_ref, *rest, tiles_per_seq, cd):
    o_refs, ext_ref = rest[:-1], rest[-1]
    tm = x_ref.shape[0]
    first = (pl.program_id(0) % tiles_per_seq) == 0
    x_ext = jnp.concatenate([prev_ref[...], x_ref[...]], axis=0)
    h = _rms(x_ext, g_ref[...]).astype(BF16)
    cv = _dot(h, w_ref[:, 0:3 * cd])
    u = cv[:, cd:2 * cd] * cv[:, 2 * cd:3 * cd]
    ext_ref[...] = u
    ext_ref[0:HALO, :] = jnp.where(first, 0.0, u[0:HALO])
    w = cw_ref[...]
    y = (w[0:1] * ext_ref[HALO - 2:HALO - 2 + tm, :] + w[1:2] * ext_ref[HALO - 1:HALO - 1 + tm, :]
         + w[2:3] * u[HALO:])
    yc_ref[...] = (cv[HALO:, 0:cd] * y).astype(yc_ref.dtype)
    off = 3 * cd
    for o_ref in o_refs:
        width = o_ref.shape[1]
        o_ref[...] = _dot(h, w_ref[:, off:off + width])[HALO:].astype(o_ref.dtype)
        off += width


def _proj_conv(x, g, w, conv_w, outs, seq, tm=ROW_TILE):
    n, d = x.shape
    cd = conv_w.shape[1]
    kern = functools.partial(_proj_conv_kernel, tiles_per_seq=seq // tm, cd=cd)
    return pl.pallas_call(
        kern,
        grid=(n // tm,),
        in_specs=[
            pl.BlockSpec((tm, d), lambda i: (i, 0)),
            pl.BlockSpec((HALO, d), lambda i: (jnp.maximum(i * (tm // HALO) - 1, 0), 0)),
            pl.BlockSpec((1, d), lambda i: (0, 0)),
            pl.BlockSpec(w.shape, lambda i: (0, 0)),
            pl.BlockSpec(conv_w.shape, lambda i: (0, 0)),
        ],
        out_specs=[pl.BlockSpec((tm, cd), lambda i: (i, 0))]
        + [pl.BlockSpec((tm, wd), lambda i: (i, 0)) for wd, _ in outs],
        out_shape=[jax.ShapeDtypeStruct((n, cd), BF16)]
        + [jax.ShapeDtypeStruct((n, wd), dt) for wd, dt in outs],
        scratch_shapes=[pltpu.VMEM((tm + HALO, cd), F32)],
        compiler_params=_params("parallel"),
        name="proj_conv",
    )(x, x, g.reshape(1, d), w, conv_w)


def _compress_kernel(kvr_ref, w1c_ref, w1_ref, w2_ref, pe_ref, o_ref, *, tok_stride, group_stride):
    nch = kvr_ref.shape[0]
    width = w1c_ref.shape[2]
    hidden = w1_ref.shape[2]
    biases = []
    for t in range(2):
        pe = jnp.broadcast_to(pe_ref[t], (8, pe_ref.shape[2])).astype(BF16)
        biases.append(_dot(pe, w1_ref[t])[0:1])
    bias = jnp.concatenate(biases, axis=1)
    for gi in range(NSA_KV_HEADS):
        a = jnp.zeros((nch, 2 * hidden), F32)
        b = jnp.zeros((nch, 2 * hidden), F32)
        for l in range(CMP_STRIDE):
            off = l * tok_stride + gi * group_stride
            tok = kvr_ref[:, off:off + width]
            a = a + _dot(tok, w1c_ref[0, l])
            b = b + _dot(tok, w1c_ref[1, l])
        act = jax.nn.gelu(a + pltpu.roll(b, nch - 1, 0) + bias).astype(BF16)
        out = [_dot(act[:, t * hidden:(t + 1) * hidden], w2_ref[t]) for t in range(2)]
        o_ref[gi * nch:(gi + 1) * nch, :] = jnp.concatenate(out, axis=1).astype(o_ref.dtype)


def _compress(kvr, w1c, w1, w2, pe, nch, tok_stride, group_stride):
    batch = kvr.shape[0] // nch
    d = w2.shape[2]
    kern = functools.partial(_compress_kernel, tok_stride=tok_stride, group_stride=group_stride)
    return pl.pallas_call(
        kern,
        grid=(batch,),
        in_specs=[
            pl.BlockSpec((nch, kvr.shape[1]), lambda i: (i, 0)),
            pl.BlockSpec(w1c.shape, lambda i: (0, 0, 0, 0)),
            pl.BlockSpec(w1.shape, lambda i: (0, 0, 0)),
            pl.BlockSpec(w2.shape, lambda i: (0, 0, 0)),
            pl.BlockSpec(pe.shape, lambda i: (0, 0, 0)),
        ],
        out_specs=pl.BlockSpec((NSA_KV_HEADS * nch, 2 * d), lambda i: (i, 0)),
        out_shape=jax.ShapeDtypeStruct((batch * NSA_KV_HEADS * nch, 2 * d), BF16),
        compiler_params=_params("parallel"),
        name="compress",
    )(kvr, w1c, w1, w2, pe)


def _cmp_kernel(q_ref, kvc_ref, ovl_ref, gt_ref, o_ref, sel_ref, *, nblk):
    tq = q_ref.shape[0]
    nc = kvc_ref.shape[0]
    d = HEAD_DIM
    i = pl.program_id(2)
    q = q_ref[...]
    kvc = kvc_ref[...]
    kc = kvc[:, 0:d]
    kvc_t = kvc.astype(F32).T.astype(BF16)
    cidx = lax.broadcasted_iota(jnp.int32, (nc, tq), 0)
    pos = i * tq + lax.broadcasted_iota(jnp.int32, (nc, tq), 1)
    cmask = cidx * CMP_STRIDE + (CMP_BLOCK - 1) <= pos
    gate = jax.nn.sigmoid(gt_ref[...]).T
    imp = jnp.zeros((LANES, tq), F32)
    outs = []
    for r in range(NSA_GROUP):
        s = lax.dot_general(kc, q[:, r * d:(r + 1) * d], _NT, preferred_element_type=F32)
        s = jnp.where(cmask, s, NEG)
        e = jnp.exp2(s - jnp.max(s, axis=0, keepdims=True))
        p = jnp.where(cmask, e / jnp.sum(e, axis=0, keepdims=True), 0.0)
        pb = p.astype(BF16)
        o = _dot(kvc_t, pb)[d:2 * d]
        outs.append(o * gate[3 * r:3 * r + 1])
        imp = imp + _dot(ovl_ref[...], pb)
    o_ref[...] = jnp.concatenate(outs, axis=0).T

    blk = lax.broadcasted_iota(jnp.int32, (LANES, tq), 0)
    posb = i * tq + lax.broadcasted_iota(jnp.int32, (LANES, tq), 1)
    cur = posb >> SEL_SHIFT
    forced = jnp.where(blk == 0, 1.0, jnp.where(blk == cur, 1.0, jnp.where(blk == cur - 1, 1.0, 0.0)))
    score = jnp.where(blk * SEL_BLOCK <= posb, imp + forced * FORCE_BONUS, NEG)
    sc = score[0:nblk, :]
    grp = 8
    sub = lax.broadcasted_iota(jnp.int32, (grp, tq), 0)
    groups = [sc[g0:g0 + grp, :] for g0 in range(0, nblk, grp)]
    ranks = [jnp.zeros((grp, tq), F32) for _ in groups]
    for sp in range(nblk):
        other = sc[sp:sp + 1, :]
        for gi, mine in enumerate(groups):
            g0 = gi * grp
            ge = jnp.where(other >= mine, 1.0, 0.0)
            gt = jnp.where(other > mine, 1.0, 0.0)
            if g0 > sp:
                beats = ge
            elif g0 + grp - 1 < sp:
                beats = gt
            else:
                beats = jnp.where(sub + g0 > sp, ge, gt)
            ranks[gi] = ranks[gi] + beats
    chosen = jnp.where(jnp.concatenate(ranks, axis=0) < SEL_TOPK, 1.0, 0.0)
    chosen = jnp.concatenate([chosen, jnp.zeros((LANES - nblk, tq), F32)], axis=0).astype(sel_ref.dtype)
    tile = sel_ref.shape[2]
    for t in range(tq // tile):
        sel_ref[t] = chosen[:, t * tile:(t + 1) * tile]


def _cmp_branch(q, kvc, ovl, gates, batch, seq, tq=CMP_TILE, sel_tile=ATTN_TILE):
    n = q.shape[0]
    nt = seq // tq
    g = NSA_KV_HEADS
    wq = NSA_GROUP * HEAD_DIM
    nc = kvc.shape[0] // (batch * g)
    kern = functools.partial(_cmp_kernel, nblk=seq // SEL_BLOCK)
    return pl.pallas_call(
        kern,
        grid=(batch, g, nt),
        in_specs=[
            pl.BlockSpec((tq, wq), lambda b, h, i: (b * nt + i, h)),
            pl.BlockSpec((nc, 2 * HEAD_DIM), lambda b, h, i: (b * g + h, 0)),
            pl.BlockSpec(ovl.shape, lambda b, h, i: (0, 0)),
            pl.BlockSpec((tq, LANES), lambda b, h, i: (b * nt + i, h)),
        ],
        out_specs=[
            pl.BlockSpec((tq, wq), lambda b, h, i: (b * nt + i, h)),
            pl.BlockSpec((tq // sel_tile, LANES, sel_tile), lambda b, h, i: ((b * g + h) * nt + i, 0, 0)),
        ],
        out_shape=[
            jax.ShapeDtypeStruct((n, g * wq), F32),
            jax.ShapeDtypeStruct((batch * g * seq // sel_tile, LANES, sel_tile), BF16),
        ],
        compiler_params=_params("parallel", "parallel", "parallel"),
        name="nsa_cmp_select",
    )(q, kvc, ovl, gates)


NSA_STAGES = 3
NSA_CHUNK = 32
MASK_BIG = 1e30


def _nsa_attn_kernel(ti_ref, tj_ref, tm_ref, tf_ref, q_ref, kv_ref, *rest, mode, branch, tq, n_units):
    if mode == "sel":
        (selt_ref, gt_ref, o_ref, qh_ref, kk_ref, vat_ref, ext_ref, bias_ref, s_ref, p_ref, al_ref, m_ref,
         acc_ref, fin_ref) = rest
    else:
        gt_ref, o_ref, qh_ref, kk_ref, vat_ref, bias_ref, s_ref, p_ref, al_ref, m_ref, acc_ref, fin_ref = rest
    tk = tq
    d = HEAD_DIM
    seq = q_ref.shape[0]
    key = lax.broadcasted_iota(jnp.int32, (tk, tq), 0)
    qry = lax.broadcasted_iota(jnp.int32, (tk, tq), 1)
    bias_ref[0] = jnp.zeros((tk, tq), F32)
    bias_ref[1] = jnp.where(key <= qry, 0.0, NEG)
    bias_ref[2] = jnp.where(key > qry, 0.0, NEG)
    s_ref[...] = jnp.zeros_like(s_ref)
    p_ref[...] = jnp.zeros_like(p_ref)
    al_ref[...] = jnp.zeros_like(al_ref)
    acc_ref[...] = jnp.zeros_like(acc_ref)
    m_ref[...] = jnp.full(m_ref.shape, NEG, F32)
    lane = lax.broadcasted_iota(jnp.int32, (tq, 2 * d), 1)
    erow = lax.broadcasted_iota(jnp.int32, (tk, LANES), 0) >> SEL_SHIFT
    ecol = lax.broadcasted_iota(jnp.int32, (tk, LANES), 1)

    def prep(c, carry):
        rows = pl.ds(pl.multiple_of(c * tq, tq), tq)
        q = q_ref[rows, :]
        for r in range(NSA_GROUP):
            pair = q[:, (r // 2) * 2 * d:(r // 2 + 1) * 2 * d]
            mine = (lane < d) if r % 2 == 0 else (lane >= d)
            qh_ref[r, rows, :] = jnp.where(mine, pair, 0.0).astype(BF16)
        kv = kv_ref[rows, :]
        k_first = kv[:, 0:d]
        kk_ref[rows, :] = jnp.concatenate([k_first, k_first], axis=1)
        v_first = jnp.concatenate([kv[:, d:2 * d], k_first], axis=1)
        vat_ref[c] = jnp.where(lane < d, v_first, 1.0).astype(F32).T.astype(BF16)
        if mode == "sel":
            ext_ref[c] = jnp.where(ecol == c * (tk // SEL_BLOCK) + erow, 1.0, 0.0).astype(BF16)
        return carry

    lax.fori_loop(0, seq // tq, prep, 0)

    def unit(s, delay):
        u = jnp.clip(s - delay, 0, n_units - 1)
        return ti_ref[u], tj_ref[u], tm_ref[u], tf_ref[u]

    def step(s, carry):
        i_c, j_c, _, f_c = unit(s, 2)
        vat = vat_ref[j_c]
        keep = jnp.where((f_c & 1) == 1, 0.0, 1.0)
        for r in range(NSA_GROUP):
            acc = acc_ref[r] * (al_ref[r] * keep) + _dot(vat, p_ref[r])
            acc_ref[r] = acc
            fin_ref[i_c, r] = acc
        _, _, _, f_b = unit(s, 1)
        first_b = (f_b & 1) == 1
        for r in range(NSA_GROUP):
            m_prev = jnp.where(first_b, NEG, m_ref[r])
            top = None
            for c in range(tk // NSA_CHUNK):
                rows = slice(c * NSA_CHUNK, (c + 1) * NSA_CHUNK)
                part = jnp.max(s_ref[r, rows, :].reshape(NSA_CHUNK // 8, 8, tq), axis=0)
                top = part if top is None else jnp.maximum(top, part)
            m_new = jnp.maximum(m_prev, jnp.max(top, axis=0, keepdims=True))
            al_ref[r] = jnp.exp2(m_prev - m_new)
            m_ref[r] = m_new
            for c in range(tk // NSA_CHUNK):
                rows = slice(c * NSA_CHUNK, (c + 1) * NSA_CHUNK)
                p_ref[r, rows, :] = jnp.exp2(s_ref[r, rows, :] - m_new).astype(BF16)
        i_a, j_a, t_a, _ = unit(s, 0)
        q_rows = pl.ds(pl.multiple_of(i_a * tq, tq), tq)
        if mode == "sel":
            chosen = _dot(ext_ref[j_a], selt_ref[i_a])
            bias_ref[3] = bias_ref[t_a] + (chosen - 1.0) * MASK_BIG
            t_a = 3
        k = kk_ref[pl.ds(pl.multiple_of(j_a * tk, tk), tk), :]
        for r in range(NSA_GROUP):
            s_ref[r] = (lax.dot_general(k, qh_ref[r, q_rows, :], _NT, preferred_element_type=F32)
                        + bias_ref[t_a])
        return carry

    lax.fori_loop(0, n_units + NSA_STAGES - 1, step, 0, unroll=3 if mode == "win" else 4)

    def finish(c, carry):
        o_rows = pl.ds(pl.multiple_of(c * tq, tq), tq)
        gate = jax.nn.sigmoid(gt_ref[o_rows, :]).T
        outs = []
        for r in range(NSA_GROUP):
            acc = fin_ref[c, r]
            outs.append(acc[0:d] / acc[d:d + 1] * gate[3 * r + branch:3 * r + branch + 1])
        o_ref[o_rows, :] = jnp.concatenate(outs, axis=0).T
        return carry

    lax.fori_loop(0, seq // tq, finish, 0)


def _nsa_attn(q, kv, selt, gates, batch, seq, mode, tq=ATTN_TILE):
    n = q.shape[0]
    nt = seq // tq
    g = NSA_KV_HEADS
    wq = NSA_GROUP * HEAD_DIM
    branch = 1 if mode == "sel" else 2
    assert WINDOW == 2 * tq
    if mode == "sel":
        units = [(i, j, int(j == i), int(j == 0) | 2 * int(j == i)) for i in range(nt) for j in range(i + 1)]
    else:
        units = [(i, j, (1, 0, 2)[i - j], int(j == max(i - 2, 0)) | 2 * int(j == i))
                 for i in range(nt) for j in range(max(i - 2, 0), i + 1)]
    tables = [jnp.array([u[c] for u in units], jnp.int32) for c in range(4)]
    kern = functools.partial(_nsa_attn_kernel, mode=mode, branch=branch, tq=tq, n_units=len(units))
    in_specs = [
        pl.BlockSpec((seq, wq), lambda b, h, *_: (b, h)),
        pl.BlockSpec((seq, 2 * HEAD_DIM), lambda b, h, *_: (b, h * 2 + branch - 1)),
    ]
    args = [q, kv]
    scratch = [
        pltpu.VMEM((NSA_GROUP, seq, 2 * HEAD_DIM), BF16),
        pltpu.VMEM((seq, 2 * HEAD_DIM), BF16),
        pltpu.VMEM((nt, 2 * HEAD_DIM, tq), BF16),
    ]
    if mode == "sel":
        in_specs.append(pl.BlockSpec((nt, LANES, tq), lambda b, h, *_: (b * g + h, 0, 0)))
        args.append(selt)
        scratch.append(pltpu.VMEM((nt, tq, LANES), BF16))
    in_specs.append(pl.BlockSpec((seq, LANES), lambda b, h, *_: (b, h)))
    args.append(gates)
    scratch += [
        pltpu.VMEM((4, tq, tq), F32),
        pltpu.VMEM((NSA_GROUP, tq, tq), F32),
        pltpu.VMEM((NSA_GROUP, tq, tq), BF16),
        pltpu.VMEM((NSA_GROUP, 1, tq), F32),
        pltpu.VMEM((NSA_GROUP, 1, tq), F32),
        pltpu.VMEM((NSA_GROUP, 2 * HEAD_DIM, tq), F32),
        pltpu.VMEM((nt, NSA_GROUP, 2 * HEAD_DIM, tq), F32),
    ]
    grid_spec = pltpu.PrefetchScalarGridSpec(
        num_scalar_prefetch=4,
        grid=(batch, g),
        in_specs=in_specs,
        out_specs=pl.BlockSpec((seq, wq), lambda b, h, *_: (b, h)),
        scratch_shapes=scratch,
    )
    return pl.pallas_call(
        kern,
        grid_spec=grid_spec,
        out_shape=jax.ShapeDtypeStruct((n, g * wq), F32),
        compiler_params=_params("parallel", "parallel"),
        name="nsa_" + mode,
    )(*tables, *args)


SB_STAGES = 4
SB_CHUNK = 32


def _sb_kernel(ti_ref, tj_ref, q_ref, k_ref, v_ref, o_ref, qh_ref, vt_ref, later_ref, bias_ref,
               z_ref, hl_ref, zl_ref, a_ref, acc_ref, carry_ref, fin_ref, *, tq, n_units):
    tk = tq
    d = HEAD_DIM
    seq = q_ref.shape[0]
    key = lax.broadcasted_iota(jnp.int32, (tk, tq), 0)
    qry = lax.broadcasted_iota(jnp.int32, (tk, tq), 1)
    later_ref[...] = jnp.where(qry > key, 1.0, 0.0).astype(BF16)
    bias_ref[0] = jnp.zeros((tk, tq), F32)
    bias_ref[1] = jnp.where(key < qry, 0.0, NEG)
    z_ref[...] = jnp.zeros_like(z_ref)
    hl_ref[...] = jnp.zeros_like(hl_ref)
    zl_ref[...] = jnp.zeros_like(zl_ref)
    a_ref[...] = jnp.zeros_like(a_ref)
    acc_ref[...] = jnp.zeros_like(acc_ref)
    carry_ref[...] = jnp.zeros_like(carry_ref)

    lane = lax.broadcasted_iota(jnp.int32, (tq, 2 * d), 1)

    def split(c, carry):
        rows = pl.ds(pl.multiple_of(c * tq, tq), tq)
        q = q_ref[rows, :]
        qh_ref[0, rows, :] = jnp.where(lane < d, q, 0.0).astype(BF16)
        qh_ref[1, rows, :] = jnp.where(lane >= d, q, 0.0).astype(BF16)
        vt_ref[c] = v_ref[rows, :].astype(F32).T.astype(BF16)
        return carry

    lax.fori_loop(0, seq // tq, split, 0)
    feat = lax.broadcasted_iota(jnp.int32, (2 * d, tq), 0)

    def unit(s, delay):
        u = jnp.clip(s - delay, 0, n_units - 1)
        return ti_ref[u], tj_ref[u]

    def step(s, carry):
        i_e, j_e = unit(s, 3)
        vt = vt_ref[j_e]
        keep = jnp.where(i_e == j_e, 0.0, 1.0)
        accs = []
        for h in range(2):
            acc = acc_ref[h] * keep + _dot(vt, a_ref[h])
            acc_ref[h] = acc
            accs.append(acc)
        fin_ref[i_e] = jnp.where(feat < d, accs[0], accs[1])
        for h in range(2):
            between = _dot(later_ref[...], hl_ref[h])
            for c in range(tk // SB_CHUNK):
                rows = slice(c * SB_CHUNK, (c + 1) * SB_CHUNK)
                a_ref[h, rows, :] = jnp.exp2(zl_ref[h, rows, :] + between[rows]).astype(BF16)
        i_b, j_b = unit(s, 1)
        keep_b = jnp.where(i_b == j_b, 0.0, 1.0)
        for h in range(2):
            before = carry_ref[h] * keep_b
            part = jnp.zeros((8, tq), F32)
            for c in range(tk // SB_CHUNK):
                rows = slice(c * SB_CHUNK, (c + 1) * SB_CHUNK)
                z = z_ref[h, rows, :]
                nz = -z
                log_stay = jnp.minimum(nz, 0.0) - jnp.log(1.0 + jnp.exp2(jnp.minimum(z, nz))) * LOG2_E
                hl_ref[h, rows, :] = log_stay.astype(BF16)
                zl_ref[h, rows, :] = z + log_stay + before
                part = part + jnp.sum(log_stay.reshape(SB_CHUNK // 8, 8, tq), axis=0)
            carry_ref[h] = before + jnp.sum(part, axis=0, keepdims=True)
        i_a, j_a = unit(s, 0)
        bias = bias_ref[jnp.where(i_a == j_a, 1, 0)]
        q_rows = pl.ds(pl.multiple_of(i_a * tq, tq), tq)
        k_rows = pl.ds(pl.multiple_of(j_a * tk, tk), tk)
        for h in range(2):
            z_ref[h] = lax.dot_general(k_ref[k_rows, :], qh_ref[h, q_rows, :], _NT,
                                       preferred_element_type=F32) + bias
        return carry

    lax.fori_loop(0, n_units + SB_STAGES - 1, step, 0, unroll=4)

    def finish(c, carry):
        o_ref[pl.ds(pl.multiple_of(c * tq, tq), tq), :] = fin_ref[c].T.astype(o_ref.dtype)
        return carry

    lax.fori_loop(0, seq // tq, finish, 0)


def _stick_breaking(qkv, batch, seq, heads, tq=ATTN_TILE):
    n = qkv.shape[0]
    nt = seq // tq
    pairs = heads // 2
    units = [(i, j) for i in range(nt) for j in range(i, -1, -1)]
    ti = jnp.array([u[0] for u in units], jnp.int32)
    tj = jnp.array([u[1] for u in units], jnp.int32)
    kern = functools.partial(_sb_kernel, tq=tq, n_units=len(units))
    grid_spec = pltpu.PrefetchScalarGridSpec(
        num_scalar_prefetch=2,
        grid=(batch, pairs),
        in_specs=[
            pl.BlockSpec((seq, LANES), lambda b, h, ti, tj: (b, h)),
            pl.BlockSpec((seq, LANES), lambda b, h, ti, tj: (b, pairs + h)),
            pl.BlockSpec((seq, LANES), lambda b, h, ti, tj: (b, 2 * pairs + h)),
        ],
        out_specs=pl.BlockSpec((seq, LANES), lambda b, h, ti, tj: (b, h)),
        scratch_shapes=[
            pltpu.VMEM((2, seq, 2 * HEAD_DIM), BF16),
            pltpu.VMEM((nt, 2 * HEAD_DIM, tq), BF16),
            pltpu.VMEM((tq, tq), BF16),
            pltpu.VMEM((2, tq, tq), F32),
            pltpu.VMEM((2, tq, tq), F32),
            pltpu.VMEM((2, tq, tq), BF16),
            pltpu.VMEM((2, tq, tq), F32),
            pltpu.VMEM((2, tq, tq), BF16),
            pltpu.VMEM((2, 2 * HEAD_DIM, tq), F32),
            pltpu.VMEM((2, 1, tq), F32),
            pltpu.VMEM((nt, 2 * HEAD_DIM, tq), F32),
        ],
    )
    return pl.pallas_call(
        kern,
        grid_spec=grid_spec,
        out_shape=jax.ShapeDtypeStruct((n, heads * HEAD_DIM), BF16),
        compiler_params=_params("parallel", "parallel"),
        name="stick_breaking",
    )(ti, tj, qkv, qkv, qkv)


def _overlap_matrix(nch, nblk):
    c0 = jnp.arange(nch) * CMP_STRIDE
    s0 = jnp.arange(LANES) * SEL_BLOCK
    lo = jnp.maximum(c0[:, None], s0[None, :])
    hi = jnp.minimum(c0[:, None] + CMP_BLOCK, s0[None, :] + SEL_BLOCK)
    ovl = jnp.maximum(hi - lo, 0).astype(F32) / CMP_BLOCK
    keep = (jnp.arange(nch)[:, None] < nch - 1) & (jnp.arange(LANES)[None, :] < nblk)
    return jnp.where(keep, ovl, 0.0).astype(BF16).T


def _conv_nsa_mixer(x, g, w_in, conv_w, pe_k, w1_k, w2_k, pe_v, w1_v, w2_v, w_out, batch, seq):
    d = HEAD_DIM
    cd = conv_w.shape[1]
    kvd = NSA_KV_HEADS * d
    o_q = 3 * cd
    o_kv = o_q + NSA_KV_HEADS * NSA_GROUP * d
    o_g = o_kv + 6 * kvd
    ng = NSA_GROUP * 3
    cmp_cols, kv_cols, gate_cols = [], [], []
    for h in range(NSA_KV_HEADS):
        for typ in range(3):
            base = o_kv + typ * 2 * kvd + h * d
            pair = [w_in[:, base:base + d], w_in[:, base + kvd:base + kvd + d]]
            if typ == 0:
                cmp_cols += pair
            else:
                kv_cols += pair
        gate_cols += [w_in[:, o_g + h * ng:o_g + (h + 1) * ng], jnp.zeros((w_in.shape[0], LANES - ng), F32)]
    w_all = jnp.concatenate(
        [w_in[:, :o_q], w_in[:, o_q:o_kv] * (Q_SCALE * LOG2_E)] + cmp_cols + kv_cols + gate_cols,
        axis=1).astype(BF16)
    y_conv, q, kv_cmp, kv, gates = _proj_conv(
        x, g, w_all, conv_w,
        [(o_kv - o_q, BF16), (2 * kvd, BF16), (4 * kvd, BF16), (NSA_KV_HEADS * LANES, F32)], seq)

    nch = seq // CMP_STRIDE
    hid = w1_k.shape[1]
    wk = w1_k.reshape(2, CMP_STRIDE, d, hid)
    wv = w1_v.reshape(2, CMP_STRIDE, d, hid)
    zero = jnp.zeros_like(wk)
    w1c = jnp.concatenate([jnp.concatenate([wk, zero], axis=3), jnp.concatenate([zero, wv], axis=3)],
                          axis=2).astype(BF16)
    w1 = jnp.stack([w1_k, w1_v]).astype(BF16)
    w2 = jnp.stack([w2_k, w2_v]).astype(BF16)
    pe = jnp.stack([pe_k.reshape(1, -1), pe_v.reshape(1, -1)])
    kvr = kv_cmp.reshape(batch * nch, CMP_STRIDE * kv_cmp.shape[1])
    kvc = _compress(kvr, w1c, w1, w2, pe, nch, kv_cmp.shape[1], 2 * d)

    ovl = _overlap_matrix(nch, seq // SEL_BLOCK)
    o_cmp, sel = _cmp_branch(q, kvc, ovl, gates, batch, seq)
    o_sel = _nsa_attn(q, kv, sel, gates, batch, seq, "sel")
    o_win = _nsa_attn(q, kv, sel, gates, batch, seq, "win")
    return [y_conv, o_cmp, o_sel, o_win], w_out.astype(BF16)


def _stick_breaking_mixer(x, g, w_qkv, w_out, batch, seq):
    hd = w_out.shape[0]
    w = jnp.concatenate([w_qkv[:, :hd] * (Q_SCALE * LOG2_E), w_qkv[:, hd:]], axis=1).astype(BF16)
    (qkv,) = _norm_proj(x, g, w, [(3 * hd, BF16)])
    y = _stick_breaking(qkv, batch, seq, hd // HEAD_DIM)
    return [y], w_out.astype(BF16)


def kernel(x, norm_ffn1, w_ffn1_in, w_ffn1_out, norm_mix, w_in_ab, conv_w, cmp_pe_k, cmp_w1_k, cmp_w2_k,
           cmp_pe_v, cmp_w1_v, cmp_w2_v, w_out_ab, w_qkv_sb, w_out_sb, norm_ffn2, w_ffn2_in, w_ffn2_out,
           norm_final):
    batch, seq, d_model = x.shape
    depth = norm_ffn1.shape[0]
    assert seq % ROW_TILE == 0 and seq % CMP_TILE == 0 and seq % ATTN_TILE == 0, seq
    assert seq // SEL_BLOCK <= LANES and SEL_BLOCK == 1 << SEL_SHIFT
    x = x.reshape(batch * seq, d_model)
    for layer in range(depth):
        x = _ffn(x, norm_ffn1[layer], w_ffn1_in[layer].astype(BF16), w_ffn1_out[layer].astype(BF16))
        i = layer // 2
        if layer % 2 == 0:
            mix = _conv_nsa_mixer(x, norm_mix[layer], w_in_ab[i], conv_w[i], cmp_pe_k[i], cmp_w1_k[i],
                                  cmp_w2_k[i], cmp_pe_v[i], cmp_w1_v[i], cmp_w2_v[i], w_out_ab[i], batch, seq)
        else:
            mix = _stick_breaking_mixer(x, norm_mix[layer], w_qkv_sb[i], w_out_sb[i], batch, seq)
        x = _ffn(x, norm_ffn2[layer], w_ffn2_in[layer].astype(BF16), w_ffn2_out[layer].astype(BF16), mix=mix,
                 final_gain=norm_final if layer == depth - 1 else None)
    return x.reshape(batch, seq, d_model)
```
